```python
import math
import jax
import jax.numpy as jnp
from jax import lax
import numpy as np

D_MODEL = 1024
BATCH = 8
SEQ = 4096
DEPTH = 2

GRID_W = 64
CTX_LEN = 256
EPS = 1e-6
N_MOD = 9
FFN_HIDDEN = 2816

DA_HEADS = 4
DA_DH = 64
DA_DV = 2 * DA_DH
DA_WIDTH = DA_HEADS * DA_DV
Q_BLOCK = 128
ROPE_BASE = 10000.0
ROPE_PAIRS = DA_DH // 4

HY_WIDTH = 256
HY_ORDER = 2
HY_EMB = 33
HY_BANDS = (HY_EMB - 1) // 2
HY_FILTER_HIDDEN = 64
HY_FAST_DECAY = 0.3
HY_SLOW_DECAY = 1.5
HY_TARGET = 1e-2

SG_WIDTH = 256
SG_GROUPS = 4
SG_CHUNK = 128

N_BRANCH = 3
MIX_WIDTH = DA_WIDTH + HY_WIDTH + SG_WIDTH
HY_OFF = 3 * DA_WIDTH
SG_OFF = HY_OFF + (HY_ORDER + 1) * HY_WIDTH
IN_WIDTH = SG_OFF + 2 * SG_WIDTH

kernel_name = 'hybrid_diffattn_hyena_gmlp_macaron_dit'


def rms_norm(x, gain):
    xf = x.astype(jnp.float32)
    y = xf * lax.rsqrt(jnp.mean(xf * xf, axis=-1, keepdims=True) + EPS)
    return (y * gain.astype(jnp.float32)).astype(x.dtype)


def layer_norm(x, gain, bias):
    xf = x.astype(jnp.float32)
    mu = jnp.mean(xf, axis=-1, keepdims=True)
    var = jnp.mean(jnp.square(xf - mu), axis=-1, keepdims=True)
    y = (xf - mu) * lax.rsqrt(var + EPS)
    return (y * gain.astype(jnp.float32) + bias.astype(jnp.float32)).astype(x.dtype)


def adaln_input(x, mod, i, gain):
    shift = mod[:, 3 * i][:, None]
    scale = mod[:, 3 * i + 1][:, None]
    return rms_norm(x, gain) * (1.0 + scale) + shift


def adaln_gate(mod, i):
    return mod[:, 3 * i + 2][:, None]


def macaron_ffn(xs, mod, i, gain, w_up, w_down):
    h = adaln_input(xs, mod, i, gain)
    a, b = jnp.split(h @ w_up, 2, axis=-1)
    return xs + 0.5 * adaln_gate(mod, i) * ((jax.nn.silu(a) * b) @ w_down)


def axial_rope_tables(n_tokens):
    rows = n_tokens // GRID_W
    row = jnp.repeat(jnp.arange(rows), GRID_W)
    col = jnp.tile(jnp.arange(GRID_W), rows)
    inv = ROPE_BASE ** (-jnp.arange(ROPE_PAIRS, dtype=jnp.float32) / ROPE_PAIRS)
    ang = jnp.stack([row, col], axis=-1).astype(jnp.float32)[:, :, None] * inv
    return jnp.cos(ang), jnp.sin(ang)


def apply_rope(x, cos, sin):
    shp = x.shape
    xr = x.reshape(shp[:-1] + (2, 2, ROPE_PAIRS))
    x1, x2 = xr[..., 0, :], xr[..., 1, :]
    c = cos[None, :, None, None].astype(x.dtype)
    s = sin[None, :, None, None].astype(x.dtype)
    out = jnp.stack([x1 * c - x2 * s, x1 * s + x2 * c], axis=-2)
    return out.reshape(shp)


def split_qkv(proj, qk_gain):
    b, n = proj.shape[:2]
    q = proj[..., :DA_WIDTH].reshape(b, n, DA_HEADS, 2, DA_DH)
    k = proj[..., DA_WIDTH:2 * DA_WIDTH].reshape(b, n, DA_HEADS, 2, DA_DH)
    v = proj[..., 2 * DA_WIDTH:3 * DA_WIDTH].reshape(b, n, DA_HEADS, DA_DV)
    return rms_norm(q, qk_gain[0]), rms_norm(k, qk_gain[1]), v


def diff_softmax_attend(q, k, v, lam):
    s = jnp.einsum('bqhcd,bkhcd->bhcqk', q, k).astype(jnp.float32) * (DA_DH ** -0.5)
    p = jax.nn.softmax(s, axis=-1)
    a = (p[:, :, 0] - lam * p[:, :, 1]).astype(v.dtype)
    return jnp.einsum('bhqk,bkhe->bqhe', a, v)


def diff_attention(pl, pc, cos, sin, qk_gain, lam_vecs, subln, layer_idx, with_ctx):
    lam_init = 0.8 - 0.6 * math.exp(-0.3 * layer_idx)
    lv = lam_vecs.astype(jnp.float32)
    lam = jnp.exp(jnp.sum(lv[0] * lv[1])) - jnp.exp(jnp.sum(lv[2] * lv[3])) + lam_init
    ql, kl, vl = split_qkv(pl, qk_gain)
    qc, kc, vc = split_qkv(pc, qk_gain)
    ql = apply_rope(ql, cos, sin)
    kl = apply_rope(kl, cos, sin)
    k_all = jnp.concatenate([kc, kl], axis=1)
    v_all = jnp.concatenate([vc, vl], axis=1)
    b, n = pl.shape[:2]
    nb = n // Q_BLOCK
    qb = ql.reshape(b, nb, Q_BLOCK, DA_HEADS, 2, DA_DH).swapaxes(0, 1)
    ol = lax.map(lambda qi: diff_softmax_attend(qi, k_all, v_all, lam), qb)
    ol = ol.swapaxes(0, 1).reshape(b, n, DA_HEADS, DA_DV)

    def head_out(o):
        return (rms_norm(o, subln) * (1.0 - lam_init)).reshape(o.shape[0], o.shape[1], DA_WIDTH)

    yl = head_out(ol)
    yc = head_out(diff_softmax_attend(qc, kc, vc, lam)) if with_ctx else None
    return yl, yc


def short_conv(u, w, b):
    up = jnp.pad(u, ((0, 0), (1, 1), (0, 0)))
    return up[:, :-2] * w[0] + up[:, 1:-1] * w[1] + up[:, 2:] * w[2] + b


def hyena_filters(n, w1, b1, w2, b2, freq, w3):
    f32 = jnp.float32
    t = jnp.linspace(0.0, 1.0, n, dtype=f32)[:, None]
    w = (2.0 * math.pi / n) * jnp.arange(n, dtype=f32)[:, None]
    bands = jnp.linspace(1e-4, HY_BANDS - 1, HY_BANDS, dtype=f32)
    z = jnp.concatenate([t, jnp.cos(bands * w), -jnp.sin(bands * w)], axis=-1)
    h = jnp.sin(freq * (z @ w1 + b1))
    h = jnp.sin(freq * (h @ w2 + b2))
    h = (h @ w3).astype(f32).reshape(n, 2, HY_ORDER, HY_WIDTH)
    max_decay = math.log(HY_TARGET) / HY_FAST_DECAY
    min_decay = math.log(HY_TARGET) / HY_SLOW_DECAY
    deltas = jnp.linspace(min_decay, max_decay, HY_ORDER * HY_WIDTH, dtype=f32).reshape(HY_ORDER, HY_WIDTH)
    h = h * jnp.exp(-t[:, :, None] * jnp.abs(deltas))[:, None]
    filt = jnp.concatenate([h[:, 0], jnp.zeros((1, HY_ORDER, HY_WIDTH), f32), jnp.flip(h[1:, 1], axis=0)], axis=0)
    return filt / jnp.sum(jnp.abs(filt), axis=0, keepdims=True)


def bidir_fftconv(u, filt, bias):
    n = u.shape[1]
    uf = jnp.fft.rfft(u.astype(jnp.float32), n=2 * n, axis=1)
    ff = jnp.fft.rfft(filt, n=2 * n, axis=0)
    y = jnp.fft.irfft(uf * ff[None], n=2 * n, axis=1)[:, :n]
    return (y + u.astype(jnp.float32) * bias).astype(u.dtype)


def hyena_branch(proj, conv_w, conv_b, filt_params, skip):
    n = proj.shape[1]
    u = short_conv(proj, conv_w, conv_b)
    v, x1, x2 = jnp.split(u, 3, axis=-1)
    filt = hyena_filters(n, *filt_params)
    z = x1 * bidir_fftconv(v, filt[:, 0], skip[0])
    return x2 * bidir_fftconv(z, filt[:, 1], skip[1])


def sgu_branch(proj, ln_g, ln_b, w_s, b_s):
    b, n = proj.shape[:2]
    u, v = jnp.split(jax.nn.gelu(proj, approximate=False), 2, axis=-1)
    v = layer_norm(v, ln_g, ln_b)
    v = v.reshape(b, n // SG_CHUNK, SG_CHUNK, SG_GROUPS, SG_WIDTH // SG_GROUPS)
    v = jnp.einsum('gij,bnjgc->bnigc', w_s, v) + jnp.transpose(b_s)[None, None, :, :, None]
    return u * v.reshape(b, n, SG_WIDTH)


def merge_branches(h, ya, yb, yc, gate_w, gate_b, w_br, w_o):
    ga, gb, gc = jnp.split(jax.nn.sigmoid(h @ gate_w + gate_b), N_BRANCH, axis=-1)
    m = (ga * (ya @ w_br[:DA_WIDTH])
         + gb * (yb @ w_br[DA_WIDTH:DA_WIDTH + HY_WIDTH])
         + gc * (yc @ w_br[DA_WIDTH + HY_WIDTH:]))
    return m @ w_o


def setup_inputs(seed: int = 0) -> dict:
    key = jax.random.key(seed)
    ks = iter(jax.random.split(key, 32))
    f32 = jnp.float32

    def nrm(shape, scale):
        return jax.random.normal(next(ks), shape, f32) * scale

    L = DEPTH
    return {
        'x': nrm((BATCH, SEQ, D_MODEL), 1.0),
        'c': nrm((BATCH, D_MODEL), 1.0),
        'ctx': nrm((BATCH, CTX_LEN, D_MODEL), 1.0),
        'c_ctx': nrm((D_MODEL,), 1.0),
        'ada_w': nrm((L, D_MODEL, N_MOD * D_MODEL), D_MODEL ** -0.5),
        'ada_b': nrm((L, N_MOD * D_MODEL), 0.02),
        'norm_g': 1.0 + nrm((L, 3, D_MODEL), 0.02),
        'ffn_up': nrm((L, 2, D_MODEL, 2 * FFN_HIDDEN), D_MODEL ** -0.5),
        'ffn_down': nrm((L, 2, FFN_HIDDEN, D_MODEL), FFN_HIDDEN ** -0.5),
        'w_in': nrm((L, D_MODEL, IN_WIDTH), D_MODEL ** -0.5),
        'da_qk_gain': 1.0 + nrm((L, 2, DA_DH), 0.02),
        'da_lambda': nrm((L, 4, DA_DH), 0.1),
        'da_subln': 1.0 + nrm((L, DA_DV), 0.02),
        'hy_conv_w': nrm((L, 3, (HY_ORDER + 1) * HY_WIDTH), 0.5),
        'hy_conv_b': nrm((L, (HY_ORDER + 1) * HY_WIDTH), 0.02),
        'hy_f_w1': nrm((L, HY_EMB, HY_FILTER_HIDDEN), HY_EMB ** -0.5),
        'hy_f_b1': nrm((L, HY_FILTER_HIDDEN), 0.1),
        'hy_f_w2': nrm((L, HY_FILTER_HIDDEN, HY_FILTER_HIDDEN), HY_FILTER_HIDDEN ** -0.5),
        'hy_f_b2': nrm((L, HY_FILTER_HIDDEN), 0.1),
        'hy_f_freq': 1.0 + nrm((L, HY_FILTER_HIDDEN), 0.05),
        'hy_f_w3': nrm((L, HY_FILTER_HIDDEN, 2 * HY_ORDER * HY_WIDTH), HY_FILTER_HIDDEN ** -0.5),
        'hy_skip': nrm((L, HY_ORDER, HY_WIDTH), 0.5),
        'sg_ln_g': 1.0 + nrm((L, SG_WIDTH), 0.02),
        'sg_ln_b': nrm((L, SG_WIDTH), 0.02),
        'sg_w': nrm((L, SG_GROUPS, SG_CHUNK, SG_CHUNK), SG_CHUNK ** -0.5),
        'sg_b': 1.0 + nrm((L, SG_GROUPS, SG_CHUNK), 0.02),
        'gate_w': nrm((L, D_MODEL, N_BRANCH * D_MODEL), D_MODEL ** -0.5),
        'gate_b': nrm((L, N_BRANCH * D_MODEL), 0.02),
        'w_br': nrm((L, MIX_WIDTH, D_MODEL), DA_WIDTH ** -0.5),
        'w_o': nrm((L, D_MODEL, D_MODEL), D_MODEL ** -0.5),
    }


def reference(x, c, ctx, c_ctx, ada_w, ada_b, norm_g, ffn_up, ffn_down, w_in, da_qk_gain, da_lambda, da_subln,
              hy_conv_w, hy_conv_b, hy_f_w1, hy_f_b1, hy_f_w2, hy_f_b2, hy_f_freq, hy_f_w3, hy_skip,
              sg_ln_g, sg_ln_b, sg_w, sg_b, gate_w, gate_b, w_br, w_o):
    cos, sin = axial_rope_tables(x.shape[1])
    s_lat = jax.nn.silu(c)
    s_ctx = jax.nn.silu(c_ctx)[None]
    xl, xc = x, ctx
    for l in range(DEPTH):
        last = l == DEPTH - 1
        mod_l = (s_lat @ ada_w[l] + ada_b[l]).reshape(-1, N_MOD, D_MODEL)
        mod_c = (s_ctx @ ada_w[l] + ada_b[l]).reshape(1, N_MOD, D_MODEL)

        xl = macaron_ffn(xl, mod_l, 0, norm_g[l, 0], ffn_up[l, 0], ffn_down[l, 0])
        xc = macaron_ffn(xc, mod_c, 0, norm_g[l, 0], ffn_up[l, 0], ffn_down[l, 0])

        hl = adaln_input(xl, mod_l, 1, norm_g[l, 1])
        hc = adaln_input(xc, mod_c, 1, norm_g[l, 1])
        pl = hl @ w_in[l]
        pc = hc @ w_in[l]
        filt_params = (hy_f_w1[l], hy_f_b1[l], hy_f_w2[l], hy_f_b2[l], hy_f_freq[l], hy_f_w3[l])
        ya_l, ya_c = diff_attention(pl, pc, cos, sin, da_qk_gain[l], da_lambda[l], da_subln[l], l, not last)
        yb_l = hyena_branch(pl[..., HY_OFF:SG_OFF], hy_conv_w[l], hy_conv_b[l], filt_params, hy_skip[l])
        yc_l = sgu_branch(pl[..., SG_OFF:], sg_ln_g[l], sg_ln_b[l], sg_w[l], sg_b[l])
        xl = xl + adaln_gate(mod_l, 1) * merge_branches(hl, ya_l, yb_l, yc_l, gate_w[l], gate_b[l], w_br[l], w_o[l])

        xl = macaron_ffn(xl, mod_l, 2, norm_g[l, 2], ffn_up[l, 1], ffn_down[l, 1])

        if not last:
            yb_c = hyena_branch(pc[..., HY_OFF:SG_OFF], hy_conv_w[l], hy_conv_b[l], filt_params, hy_skip[l])
            yc_c = sgu_branch(pc[..., SG_OFF:], sg_ln_g[l], sg_ln_b[l], sg_w[l], sg_b[l])
            xc = xc + adaln_gate(mod_c, 1) * merge_branches(hc, ya_c, yb_c, yc_c, gate_w[l], gate_b[l], w_br[l], w_o[l])
            xc = macaron_ffn(xc, mod_c, 2, norm_g[l, 2], ffn_up[l, 1], ffn_down[l, 1])
    return xl
```

```python
import functools
import math

import numpy as np
import jax
import jax.numpy as jnp
from jax import lax
from jax.experimental import pallas as pl
from jax.experimental.pallas import tpu as pltpu

F32 = jnp.float32
BF16 = jnp.bfloat16

D_MODEL = 1024
DEPTH = 2
GRID_W = 64
EPS = 1e-6
N_MOD = 9
FFN_HIDDEN = 2816

DA_HEADS = 4
DA_DH = 64
DA_DV = 2 * DA_DH
DA_WIDTH = DA_HEADS * DA_DV
ROPE_BASE = 10000.0
ROPE_PAIRS = DA_DH // 4

HY_WIDTH = 256
HY_ORDER = 2
HY_EMB = 33
HY_BANDS = (HY_EMB - 1) // 2
HY_FAST_DECAY = 0.3
HY_SLOW_DECAY = 1.5
HY_TARGET = 1e-2

SG_WIDTH = 256
SG_GROUPS = 4
SG_CHUNK = 128

HY_OFF = 3 * DA_WIDTH
SG_OFF = HY_OFF + (HY_ORDER + 1) * HY_WIDTH
IN_WIDTH = SG_OFF + 2 * SG_WIDTH

VMEM_LIMIT_BYTES = 56 * 1024 * 1024
LONG_CONV_VMEM_BYTES = 60 * 1024 * 1024
LANES = 128
FFT_N2 = 128
MOD_ROWS = 16
Q_SCALE = math.log2(math.e) * DA_DH ** -0.5


def _cparams(*sem, vmem=VMEM_LIMIT_BYTES):
    return pltpu.CompilerParams(dimension_semantics=sem, vmem_limit_bytes=vmem)


def _resident(shape, index_map):
    return pl.BlockSpec(shape, index_map, pipeline_mode=pl.Buffered(1))


def _dot(a, b):
    return jnp.dot(a, b, preferred_element_type=F32)


def _split(a):
    hi = a.astype(BF16)
    lo = (a - hi.astype(F32)).astype(BF16)
    return hi, lo


def _dot3(a, b):
    ah, al = _split(a)
    bh, bl = _split(b)
    return _dot(ah, bh) + (_dot(al, bh) + _dot(ah, bl))


def _sigmoid(x):
    return 1.0 / (1.0 + jnp.exp(-x))


def _adaln(x, gain, shift, scale):
    y = x * lax.rsqrt(jnp.mean(x * x, axis=-1, keepdims=True) + EPS)
    return (y * gain) * (1.0 + scale) + shift


def _mod_kernel(c_ref, w_ref, b_ref, o_ref):
    c = c_ref[...]
    s = c * _sigmoid(c)
    o_ref[0] = _dot(s.astype(BF16), w_ref[0].astype(BF16)) + b_ref[0]


def _modulation(cc, ada_w, ada_b):
    depth, d, n = ada_w.shape
    tn = 1536
    return pl.pallas_call(
        _mod_kernel,
        grid=(depth, n // tn),
        in_specs=[
            pl.BlockSpec((MOD_ROWS, d), lambda l, j: (0, 0)),
            pl.BlockSpec((1, d, tn), lambda l, j: (l, 0, j)),
            pl.BlockSpec((1, 1, tn), lambda l, j: (l, 0, j)),
        ],
        out_specs=pl.BlockSpec((1, MOD_ROWS, tn), lambda l, j: (l, 0, j)),
        out_shape=jax.ShapeDtypeStruct((depth, MOD_ROWS, n), F32),
        compiler_params=_cparams("parallel", "parallel"),
        name="modulation",
    )(cc, ada_w, ada_b.reshape(depth, 1, n))


def _ffn_kernel(x_ref, mod_ref, g_ref, wu_ref, wd_ref, o_ref, *, sub):
    x = x_ref[0]
    shift = mod_ref[0, 3 * sub:3 * sub + 1, :]
    scale = mod_ref[0, 3 * sub + 1:3 * sub + 2, :]
    gate = mod_ref[0, 3 * sub + 2:3 * sub + 3, :]
    h = _adaln(x, g_ref[...], shift, scale).astype(BF16)
    a = _dot(h, wu_ref[:, :FFN_HIDDEN])
    b = _dot(h, wu_ref[:, FFN_HIDDEN:])
    g = (a * _sigmoid(a) * b).astype(BF16)
    o_ref[0] = x + (0.5 * gate) * _dot(g, wd_ref[...])


def _ffn(x, mod, shared_mod, gain, w_up, w_down, sub):
    b, s, d = x.shape
    tm = min(s, 512)
    mod_map = (lambda i, j: (0, 0, 0)) if shared_mod else (lambda i, j: (i, 0, 0))
    return pl.pallas_call(
        functools.partial(_ffn_kernel, sub=sub),
        grid=(b, s // tm),
        in_specs=[
            pl.BlockSpec((1, tm, d), lambda i, j: (i, j, 0)),
            pl.BlockSpec((1, N_MOD, d), mod_map),
            pl.BlockSpec((1, d), lambda i, j: (0, 0)),
            _resident((d, 2 * FFN_HIDDEN), lambda i, j: (0, 0)),
            _resident((FFN_HIDDEN, d), lambda i, j: (0, 0)),
        ],
        out_specs=pl.BlockSpec((1, tm, d), lambda i, j: (i, j, 0)),
        out_shape=jax.ShapeDtypeStruct((b, s, d), F32),
        compiler_params=_cparams("parallel", "parallel"),
        name=f"ffn{sub}",
    )(x, mod, gain.reshape(1, d), w_up, w_down)


def _group_sums(sq, ones_bd):
    outs = []
    for j in range(sq.shape[1] // 256):
        hi, lo = _split(sq[:, 256 * j:256 * (j + 1)])
        outs.append(_dot(hi, ones_bd) + _dot(lo, ones_bd))
    return jnp.concatenate(outs, axis=-1)


def _swap16(x):
    n = x.shape[-1]
    lane = lax.broadcasted_iota(jnp.int32, x.shape, x.ndim - 1)
    up = pltpu.roll(x, 16, x.ndim - 1)
    dn = pltpu.roll(x, n - 16, x.ndim - 1)
    return jnp.where((lane & 16) != 0, up, dn)


def _mixer_in_kernel(x_ref, mod_ref, g_ref, w_ref, qkg_ref, cos_ref, sin_ref, ones_ref,
                     lng_ref, lnb_ref, ws_ref, bs_ref,
                     q_ref, k_ref, v_ref, hy_ref, yc_ref, *, rope):
    x = x_ref[0]
    tm = x.shape[0]
    h = _adaln(x, g_ref[...], mod_ref[0, 3:4, :], mod_ref[0, 4:5, :]).astype(BF16)

    qk = _dot(h, w_ref[:, :2 * DA_WIDTH])
    ss = _group_sums(qk * qk, ones_ref[...])
    qk = qk * lax.rsqrt(ss * (1.0 / DA_DH) + EPS) * qkg_ref[...]
    if rope:
        reps = 2 * DA_WIDTH // LANES
        cosf = jnp.concatenate([cos_ref[...]] * reps, axis=-1)
        sinf = jnp.concatenate([sin_ref[...]] * reps, axis=-1)
        qk = qk * cosf + _swap16(qk) * sinf
    q_ref[0] = (qk[:, :DA_WIDTH] * Q_SCALE).astype(BF16)
    k_ref[0] = qk[:, DA_WIDTH:].astype(BF16)

    v_ref[0] = _dot(h, w_ref[:, 2 * DA_WIDTH:HY_OFF]).astype(BF16)
    hy_ref[0] = _dot(h, w_ref[:, HY_OFF:SG_OFF])

    sg = _dot(h, w_ref[:, SG_OFF:])
    sg = 0.5 * sg * (1.0 + lax.erf(sg * (2.0 ** -0.5)))
    u = sg[:, :SG_WIDTH]
    vv = sg[:, SG_WIDTH:]
    mu = jnp.mean(vv, axis=-1, keepdims=True)
    var = jnp.mean(jnp.square(vv - mu), axis=-1, keepdims=True)
    vv = ((vv - mu) * lax.rsqrt(var + EPS) * lng_ref[...] + lnb_ref[...]).astype(BF16)
    group = lax.broadcasted_iota(jnp.int32, (SG_CHUNK, SG_WIDTH), 1) // (SG_WIDTH // SG_GROUPS)
    for c in range(tm // SG_CHUNK):
        rows = slice(c * SG_CHUNK, (c + 1) * SG_CHUNK)
        mixed = _dot(ws_ref[...], vv[rows])
        sel = mixed[:SG_CHUNK]
        for g in range(1, SG_GROUPS):
            sel = jnp.where(group == g, mixed[g * SG_CHUNK:(g + 1) * SG_CHUNK], sel)
        yc_ref[0, rows, :] = (u[rows] * (sel + bs_ref[...])).astype(BF16)


def _mixer_in(x, mod, shared_mod, gain, w_in, qk_gain_row, cos_t, sin_t, ones_bd,
              ln_g, ln_b, ws_stack, bs_tile, rope):
    b, s, d = x.shape
    tm = min(s, 512)
    mod_map = (lambda i, j: (0, 0, 0)) if shared_mod else (lambda i, j: (i, 0, 0))
    const = lambda i, j: (0, 0)
    tok = lambda i, j: (i, j, 0)
    hy_w = SG_OFF - HY_OFF
    return pl.pallas_call(
        functools.partial(_mixer_in_kernel, rope=rope),
        grid=(b, s // tm),
        in_specs=[
            pl.BlockSpec((1, tm, d), tok),
            pl.BlockSpec((1, N_MOD, d), mod_map),
            pl.BlockSpec((1, d), const),
            _resident((d, IN_WIDTH), const),
            pl.BlockSpec((1, 2 * DA_WIDTH), const),
            pl.BlockSpec((tm, LANES), lambda i, j: (j, 0)),
            pl.BlockSpec((tm, LANES), lambda i, j: (j, 0)),
            pl.BlockSpec((256, 256), const),
            pl.BlockSpec((1, SG_WIDTH), const),
            pl.BlockSpec((1, SG_WIDTH), const),
            pl.BlockSpec((SG_GROUPS * SG_CHUNK, SG_CHUNK), const),
            pl.BlockSpec((SG_CHUNK, SG_WIDTH), const),
        ],
        out_specs=[
            pl.BlockSpec((1, tm, DA_WIDTH), tok),
            pl.BlockSpec((1, tm, DA_WIDTH), tok),
            pl.BlockSpec((1, tm, DA_WIDTH), tok),
            pl.BlockSpec((1, tm, hy_w), tok),
            pl.BlockSpec((1, tm, SG_WIDTH), tok),
        ],
        out_shape=[
            jax.ShapeDtypeStruct((b, s, DA_WIDTH), BF16),
            jax.ShapeDtypeStruct((b, s, DA_WIDTH), BF16),
            jax.ShapeDtypeStruct((b, s, DA_WIDTH), BF16),
            jax.ShapeDtypeStruct((b, s, hy_w), F32),
            jax.ShapeDtypeStruct((b, s, SG_WIDTH), BF16),
        ],
        compiler_params=_cparams("parallel", "parallel"),
        name="mixer_in_rope" if rope else "mixer_in",
    )(x, mod, gain.reshape(1, d), w_in, qk_gain_row, cos_t, sin_t, ones_bd,
      ln_g.reshape(1, -1), ln_b.reshape(1, -1), ws_stack, bs_tile)


def _attn_kernel(q_ref, k_ref, v_ref, lv_ref, sub_ref, o_ref, *, lam_init):
    lv = lv_ref[...]
    lam = (jnp.exp(jnp.sum(lv[0:1] * lv[1:2], axis=-1, keepdims=True))
           - jnp.exp(jnp.sum(lv[2:3] * lv[3:4], axis=-1, keepdims=True)) + lam_init)
    q = q_ref[0]
    k = k_ref[0]
    nt = (((1,), (1,)), ((), ()))
    probs = []
    for c in range(2):
        cols = slice(c * DA_DH, (c + 1) * DA_DH)
        s = lax.dot_general(q[:, cols], k[:, cols], nt, preferred_element_type=F32)
        p = jnp.exp2(s - jnp.max(s, axis=-1, keepdims=True))
        probs.append((p, jnp.sum(p, axis=-1, keepdims=True)))
    (p0, l0), (p1, l1) = probs
    a = (p0 - (lam * l0 / l1) * p1).astype(BF16)
    o = _dot(a, v_ref[0]) * (1.0 / l0)
    y = o * lax.rsqrt(jnp.mean(o * o, axis=-1, keepdims=True) + EPS)
    o_ref[0] = (y * sub_ref[...] * (1.0 - lam_init)).astype(BF16)


def _attention(q, k, v, lam_vecs, subln, layer_idx):
    b, sq, _ = q.shape
    sk = k.shape[1]
    tq = min(sq, 256)
    lam_init = 0.8 - 0.6 * math.exp(-0.3 * layer_idx)
    return pl.pallas_call(
        functools.partial(_attn_kernel, lam_init=lam_init),
        grid=(b, DA_HEADS, sq // tq),
        in_specs=[
            pl.BlockSpec((1, tq, DA_DV), lambda i, h, j: (i, j, h)),
            pl.BlockSpec((1, sk, DA_DV), lambda i, h, j: (i, 0, h)),
            pl.BlockSpec((1, sk, DA_DV), lambda i, h, j: (i, 0, h)),
            pl.BlockSpec((4, DA_DH), lambda i, h, j: (0, 0)),
            pl.BlockSpec((1, DA_DV), lambda i, h, j: (0, 0)),
        ],
        out_specs=pl.BlockSpec((1, tq, DA_DV), lambda i, h, j: (i, j, h)),
        out_shape=jax.ShapeDtypeStruct((b, sq, DA_WIDTH), BF16),
        compiler_params=_cparams("parallel", "parallel", "parallel"),
        name="diff_attention",
    )(q, k, v, lam_vecs, subln.reshape(1, DA_DV))


def _short_conv_kernel(p0_ref, p1_ref, p2_ref, w0_ref, w1_ref, w2_ref, b0_ref, b1_ref, b2_ref,
                       o0_ref, o1_ref, o2_ref):
    n = p0_ref.shape[1]
    row = lax.broadcasted_iota(jnp.int32, (n, LANES), 0)
    for p_ref, w_ref, b_ref, o_ref in ((p0_ref, w0_ref, b0_ref, o0_ref),
                                       (p1_ref, w1_ref, b1_ref, o1_ref),
                                       (p2_ref, w2_ref, b2_ref, o2_ref)):
        p = p_ref[0]
        prev = jnp.where(row == 0, 0.0, pltpu.roll(p, 1, 0))
        nxt = jnp.where(row == n - 1, 0.0, pltpu.roll(p, n - 1, 0))
        w = w_ref[...]
        o_ref[0, 0] = prev * w[0:1] + p * w[1:2] + nxt * w[2:3] + b_ref[...]


def _short_conv(p, conv_w, conv_b):
    b, n, _ = p.shape
    halves = HY_WIDTH // LANES
    seq = lambda part: pl.BlockSpec((1, n, LANES), lambda i, j: (i, 0, part * halves + j))
    wsp = lambda part: pl.BlockSpec((3, LANES), lambda i, j: (0, part * halves + j))
    bsp = lambda part: pl.BlockSpec((1, LANES), lambda i, j: (0, part * halves + j))
    out = pl.BlockSpec((1, 1, n, LANES), lambda i, j: (i, j, 0, 0))
    shp = jax.ShapeDtypeStruct((b, halves, n, LANES), F32)
    cb = conv_b.reshape(1, -1)
    return pl.pallas_call(
        _short_conv_kernel,
        grid=(b, halves),
        in_specs=[seq(0), seq(1), seq(2), wsp(0), wsp(1), wsp(2), bsp(0), bsp(1), bsp(2)],
        out_specs=[out, out, out],
        out_shape=[shp, shp, shp],
        compiler_params=_cparams("parallel", "parallel"),
        name="hyena_short_conv",
    )(p, p, p, conv_w, conv_w, conv_w, cb, cb, cb)


def _filter_kernel(z_ref, dec_ref, w1_ref, b1_ref, w2_ref, b2_ref, fr_ref, w3_ref, f_ref, l1_ref):
    i = pl.program_id(0)
    fr = fr_ref[...]
    h = jnp.sin(fr * (_dot3(z_ref[...], w1_ref[...]) + b1_ref[...]))
    h = jnp.sin(fr * (_dot3(h, w2_ref[...]) + b2_ref[...]))
    h = _dot3(h, w3_ref[...])
    tr = h.shape[0]
    half = HY_ORDER * HY_WIDTH
    dec = dec_ref[...]
    fwd = h[:, :half] * dec
    first = (lax.broadcasted_iota(jnp.int32, (tr, half), 0) + i * tr) == 0
    bwd = jnp.where(first, 0.0, h[:, half:] * dec)
    for o in range(HY_ORDER):
        for hl in range(HY_WIDTH // LANES):
            cols = slice(o * HY_WIDTH + hl * LANES, o * HY_WIDTH + (hl + 1) * LANES)
            f_ref[o, hl] = fwd[:, cols]
            f_ref[HY_ORDER + o, hl] = bwd[:, cols]
    part = jnp.sum(jnp.abs(fwd) + jnp.abs(bwd), axis=0, keepdims=True)

    @pl.when(i == 0)
    def _():
        l1_ref[...] = part

    @pl.when(i != 0)
    def _():
        l1_ref[...] += part


def _hyena_filter_taps(n, w1, b1, w2, b2, freq, w3):
    t = jnp.linspace(0.0, 1.0, n, dtype=F32)[:, None]
    w = (2.0 * math.pi / n) * jnp.arange(n, dtype=F32)[:, None]
    bands = jnp.linspace(1e-4, HY_BANDS - 1, HY_BANDS, dtype=F32)
    z = jnp.concatenate([t, jnp.cos(bands * w), -jnp.sin(bands * w)], axis=-1)
    emb = 48
    z = jnp.pad(z, ((0, 0), (0, emb - HY_EMB)))
    w1p = jnp.pad(w1, ((0, emb - HY_EMB), (0, 0)))
    max_decay = math.log(HY_TARGET) / HY_FAST_DECAY
    min_decay = math.log(HY_TARGET) / HY_SLOW_DECAY
    deltas = jnp.linspace(min_decay, max_decay, HY_ORDER * HY_WIDTH, dtype=F32)
    decay = jnp.exp(-t * jnp.abs(deltas)[None, :])
    tr = min(n, 512)
    hid = w2.shape[0]
    half = HY_ORDER * HY_WIDTH
    const = lambda i: (0, 0)
    return pl.pallas_call(
        _filter_kernel,
        grid=(n // tr,),
        in_specs=[
            pl.BlockSpec((tr, emb), lambda i: (i, 0)),
            pl.BlockSpec((tr, half), lambda i: (i, 0)),
            pl.BlockSpec((emb, hid), const),
            pl.BlockSpec((1, hid), const),
            pl.BlockSpec((hid, hid), const),
            pl.BlockSpec((1, hid), const),
            pl.BlockSpec((1, hid), const),
            pl.BlockSpec((hid, 2 * half), const),
        ],
        out_specs=[
            pl.BlockSpec((2 * HY_ORDER, HY_WIDTH // LANES, tr, LANES), lambda i: (0, 0, i, 0)),
            pl.BlockSpec((1, half), const),
        ],
        out_shape=[
            jax.ShapeDtypeStruct((2 * HY_ORDER, HY_WIDTH // LANES, n, LANES), F32),
            jax.ShapeDtypeStruct((1, half), F32),
        ],
        compiler_params=_cparams("arbitrary"),
        name="hyena_filter_taps",
    )(z, decay, w1p, b1.reshape(1, -1), w2, b2.reshape(1, -1), freq.reshape(1, -1), w3)


@functools.lru_cache(maxsize=None)
def _dft_tables(n1):
    n2 = FFT_N2
    n = n1 * n2
    half = n1 // 2
    ksp = -(-(half + 1) // 8) * 8
    k1 = np.arange(ksp)[:, None].astype(np.float64)
    t1 = np.arange(half)[None, :].astype(np.float64)
    ang_a = 2.0 * np.pi * k1 * t1 / n1
    live = (k1 <= half).astype(np.float64)
    wa_fwd = np.concatenate([np.cos(ang_a) * live, -np.sin(ang_a) * live], axis=0)
    weight = np.where(k1[:half] == 0, 1.0, 2.0)
    wa_inv = np.concatenate([(weight * np.cos(ang_a[:half])).T,
                             (-weight * np.sin(ang_a[:half])).T], axis=1)
    kk1 = np.arange(half + 1)[:, None, None].astype(np.float64)
    kk2 = np.arange(n2)[None, :, None].astype(np.float64)
    tt2 = np.arange(n2)[None, None, :].astype(np.float64)
    ang_b = 2.0 * np.pi * tt2 * (kk1 + n1 * kk2) / n
    er, ei = np.cos(ang_b), -np.sin(ang_b)
    mf = np.concatenate([np.concatenate([er, -ei], axis=2),
                         np.concatenate([ei, er], axis=2)], axis=1)
    mi = np.swapaxes(mf, 1, 2)
    f32 = lambda a: np.ascontiguousarray(a.astype(np.float32))
    return f32(wa_fwd), f32(wa_inv), f32(mf), f32(mi)


def _row_mix(w, x):
    k_dim = w.shape[1]
    if k_dim >= 16:
        return _dot(w, x.astype(BF16))
    w = w.astype(F32)
    x = x.astype(F32)
    out = w[:, 0:1] * x[0:1, :]
    for k in range(1, k_dim):
        out = out + w[:, k:k + 1] * x[k:k + 1, :]
    return out


def _lane_halves(refs, idx):
    return jnp.concatenate([r[idx] for r in refs], axis=-1)


def _dft_forward(u_refs, a_ref, wa_ref, mf_ref, half, emit):
    ksp = wa_ref.shape[0] // 2
    wa = wa_ref[...].astype(BF16)

    def stage_a(j, carry):
        rows = pl.ds(j, half, stride=FFT_N2)
        r = _row_mix(wa, _lane_halves(u_refs, (rows, slice(None))))
        for part in range(2):
            body = r[part * ksp:part * ksp + half]
            last = r[part * ksp + half:part * ksp + half + 1]
            for h in range(2):
                a_ref[part, h, rows, :] = body[:, h * LANES:(h + 1) * LANES]
                a_ref[part, h, pl.ds(half * FFT_N2 + j, 1), :] = last[:, h * LANES:(h + 1) * LANES]
        return carry

    lax.fori_loop(0, FFT_N2, stage_a, 0, unroll=4)

    def stage_b(k, carry):
        rows = pl.ds(pl.multiple_of(k * FFT_N2, FFT_N2), FFT_N2)
        a = jnp.concatenate([_lane_halves((a_ref.at[0, 0], a_ref.at[0, 1]), (rows, slice(None))),
                             _lane_halves((a_ref.at[1, 0], a_ref.at[1, 1]), (rows, slice(None)))], axis=0)
        emit(k, rows, _dot(mf_ref[k], a.astype(BF16)))
        return carry

    lax.fori_loop(0, half + 1, stage_b, 0)


def _spectrum_kernel(t0_ref, t1_ref, wa_ref, mf_ref, l1_ref, h_ref, a_ref, *, half, scale):
    d = pl.program_id(1)
    inv = scale / l1_ref[0]

    def emit(k, rows, x):
        xr = x[:FFT_N2] * inv
        xi = x[FFT_N2:] * inv

        @pl.when(d == 0)
        def _():
            h_ref[0, 0, k] = xr
            h_ref[0, 1, k] = xi

        @pl.when(d != 0)
        def _():
            h_ref[0, 0, k] += xr
            h_ref[0, 1, k] -= xi

    _dft_forward((t0_ref, t1_ref), a_ref, wa_ref, mf_ref, half, emit)


def _filter_spectrum(taps, wa_fwd, mf, l1, n1):
    n = taps.shape[2]
    c = HY_WIDTH
    half = n1 // 2
    ks = half + 1
    lane = lambda h: pl.BlockSpec((None, None, n, LANES), lambda o, d: (d * HY_ORDER + o, h, 0, 0))
    const2 = lambda o, d: (0, 0)
    return pl.pallas_call(
        functools.partial(_spectrum_kernel, half=half, scale=1.0 / (n1 * FFT_N2)),
        grid=(HY_ORDER, 2),
        in_specs=[lane(0), lane(1),
                  pl.BlockSpec(wa_fwd.shape, const2),
                  _resident(mf.shape, lambda o, d: (0, 0, 0)),
                  pl.BlockSpec((1, 1, c), lambda o, d: (o, 0, 0))],
        out_specs=pl.BlockSpec((1, 2, ks, FFT_N2, c), lambda o, d: (o, 0, 0, 0, 0)),
        out_shape=jax.ShapeDtypeStruct((HY_ORDER, 2, ks, FFT_N2, c), F32),
        scratch_shapes=[pltpu.VMEM((2, 2, ks * FFT_N2, LANES), F32)],
        compiler_params=_cparams("parallel", "arbitrary"),
        name="hyena_filter_spectrum",
    )(taps, taps, wa_fwd, mf, l1.reshape(HY_ORDER, 1, c))


def _long_conv_kernel(u0_ref, u1_ref, g0_ref, g1_ref, skip_ref, wa_ref, wi_ref, mf_ref, mi_ref, h_ref,
                      o_ref, a_ref, *, half):
    def emit(k, rows, x):
        xr, xi = x[:FFT_N2], x[FFT_N2:]
        hr, hi = h_ref[0, k], h_ref[1, k]
        y = jnp.concatenate([xr * hr - xi * hi, xr * hi + xi * hr], axis=0).astype(BF16)
        c = _dot(mi_ref[k], y)
        for part in range(2):
            for h in range(2):
                a_ref[part, h, rows, :] = c[part * FFT_N2:(part + 1) * FFT_N2, h * LANES:(h + 1) * LANES]

    _dft_forward((u0_ref, u1_ref), a_ref, wa_ref, mf_ref, half, emit)

    alt = (1 - 2 * (lax.broadcasted_iota(jnp.int32, (half, 1), 0) & 1)).astype(F32)
    skip = skip_ref[...]
    wi = wi_ref[...].astype(BF16)

    def stage_a_inv(j, carry):
        rows = pl.ds(j, half, stride=FFT_N2)
        idx = (rows, slice(None))
        c = jnp.concatenate([_lane_halves((a_ref.at[0, 0], a_ref.at[0, 1]), idx),
                             _lane_halves((a_ref.at[1, 0], a_ref.at[1, 1]), idx)], axis=0)
        nyq = _lane_halves((a_ref.at[0, 0], a_ref.at[0, 1]), (pl.ds(half * FFT_N2 + j, 1), slice(None)))
        y = _row_mix(wi, c) + alt * nyq
        u = _lane_halves((u0_ref, u1_ref), idx)
        g = _lane_halves((g0_ref, g1_ref), idx)
        out = g * (y + u * skip)
        o_ref[0, rows, :] = out[:, :LANES]
        o_ref[1, rows, :] = out[:, LANES:]
        return carry

    lax.fori_loop(0, FFT_N2, stage_a_inv, 0, unroll=4)


def _long_conv(u, gate, skip_row, tables, hspec, order, n1):
    b, halves, n, _ = u.shape
    c = halves * LANES
    half = n1 // 2
    ks = half + 1
    wa_fwd, wa_inv, mf, mi = tables
    lane = lambda h: pl.BlockSpec((None, None, n, LANES), lambda i: (i, h, 0, 0))
    const2 = lambda i: (0, 0)
    const3 = lambda i: (0, 0, 0)
    return pl.pallas_call(
        functools.partial(_long_conv_kernel, half=half),
        grid=(b,),
        in_specs=[lane(0), lane(1), lane(0), lane(1),
                  pl.BlockSpec((1, c), const2),
                  pl.BlockSpec(wa_fwd.shape, const2),
                  pl.BlockSpec(wa_inv.shape, const2),
                  _resident(mf.shape, const3),
                  _resident(mi.shape, const3),
                  _resident((None, 2, ks, FFT_N2, c), lambda i: (order, 0, 0, 0, 0))],
        out_specs=pl.BlockSpec((None, halves, n, LANES), lambda i: (i, 0, 0, 0)),
        out_shape=jax.ShapeDtypeStruct(u.shape, F32),
        scratch_shapes=[pltpu.VMEM((2, 2, ks * FFT_N2, LANES), F32)],
        compiler_params=_cparams("parallel", vmem=LONG_CONV_VMEM_BYTES),
        name="hyena_long_conv",
    )(u, u, gate, gate, skip_row, wa_fwd, wa_inv, mf, mi, hspec)


def _hyena(p, conv_w, conv_b, filt_params, skip, tables):
    b, n, _ = p.shape
    n1 = 2 * n // FFT_N2
    wa_fwd, _, mf, _ = tables
    v, x1, x2 = _short_conv(p, conv_w, conv_b)
    taps, l1 = _hyena_filter_taps(n, *filt_params)
    hspec = _filter_spectrum(taps, wa_fwd, mf, l1, n1)
    z = _long_conv(v, x1, skip[0:1], tables, hspec, 0, n1)
    return _long_conv(z, x2, skip[1:2], tables, hspec, 1, n1)


def _merge_kernel(x_ref, mod_ref, g_ref, ya_ref, yb_ref, yc_ref, gw_ref, gb_ref, wbr_ref, wo_ref, o_ref):
    x = x_ref[0]
    d = x.shape[1]
    h = _adaln(x, g_ref[...], mod_ref[0, 3:4, :], mod_ref[0, 4:5, :]).astype(BF16)
    gates = _sigmoid(_dot(h, gw_ref[...]) + gb_ref[...])
    yb = jnp.concatenate([yb_ref[0, hl] for hl in range(yb_ref.shape[1])], axis=-1).astype(BF16)
    m = (gates[:, :d] * _dot(ya_ref[0], wbr_ref[:DA_WIDTH])
         + gates[:, d:2 * d] * _dot(yb, wbr_ref[DA_WIDTH:DA_WIDTH + HY_WIDTH])
         + gates[:, 2 * d:] * _dot(yc_ref[0], wbr_ref[DA_WIDTH + HY_WIDTH:]))
    o_ref[0] = x + mod_ref[0, 5:6, :] * _dot(m.astype(BF16), wo_ref[...])


def _merge(x, mod, shared_mod, gain, ya, yb, yc, gate_w, gate_b, w_br, w_o):
    b, s, d = x.shape
    tm = min(s, 512)
    mod_map = (lambda i, j: (0, 0, 0)) if shared_mod else (lambda i, j: (i, 0, 0))
    const = lambda i, j: (0, 0)
    tok = lambda i, j: (i, j, 0)
    return pl.pallas_call(
        _merge_kernel,
        grid=(b, s // tm),
        in_specs=[
            pl.BlockSpec((1, tm, d), tok),
            pl.BlockSpec((1, N_MOD, d), mod_map),
            pl.BlockSpec((1, d), const),
            pl.BlockSpec((1, tm, DA_WIDTH), tok),
            pl.BlockSpec((1, HY_WIDTH // LANES, tm, LANES), lambda i, j: (i, 0, j, 0)),
            pl.BlockSpec((1, tm, SG_WIDTH), tok),
            _resident(gate_w.shape, const),
            pl.BlockSpec((1, gate_w.shape[1]), const),
            _resident(w_br.shape, const),
            _resident(w_o.shape, const),
        ],
        out_specs=pl.BlockSpec((1, tm, d), tok),
        out_shape=jax.ShapeDtypeStruct((b, s, d), F32),
        compiler_params=_cparams("parallel", "parallel"),
        name="branch_merge",
    )(x, mod, gain.reshape(1, d), ya, yb, yc, gate_w, gate_b.reshape(1, -1), w_br, w_o)


def _rope_tables(n_tokens):
    t = np.arange(n_tokens)
    pos = np.stack([t // GRID_W, t % GRID_W], axis=-1).astype(np.float64)
    inv = ROPE_BASE ** (-np.arange(ROPE_PAIRS, dtype=np.float64) / ROPE_PAIRS)
    ang = pos[:, :, None] * inv
    cos, sin = np.cos(ang), np.sin(ang)
    cos64 = np.concatenate([cos[:, 0], cos[:, 0], cos[:, 1], cos[:, 1]], axis=-1)
    sin64 = np.concatenate([-sin[:, 0], sin[:, 0], -sin[:, 1], sin[:, 1]], axis=-1)
    as_f32 = lambda a: np.tile(a, (1, LANES // DA_DH)).astype(np.float32)
    return as_f32(cos64), as_f32(sin64)


def _device_tables(n1):
    wa_fwd, wa_inv, mf, mi = _dft_tables(n1)
    return wa_fwd, wa_inv, jnp.asarray(mf).astype(BF16), jnp.asarray(mi).astype(BF16)


def kernel(x, c, ctx, c_ctx, ada_w, ada_b, norm_g, ffn_up, ffn_down, w_in, da_qk_gain, da_lambda, da_subln, hy_conv_w, hy_conv_b, hy_f_w1, hy_f_b1, hy_f_w2, hy_f_b2, hy_f_freq, hy_f_w3, hy_skip, sg_ln_g, sg_ln_b, sg_w, sg_b, gate_w, gate_b, w_br, w_o):
    batch, seq, d = x.shape
    cos_t, sin_t = _rope_tables(seq)
    ones_bd = np.kron(np.eye(256 // DA_DH), np.ones((DA_DH, DA_DH))).astype(BF16)
    tables_l = _device_tables(2 * seq // FFT_N2)
    tables_c = _device_tables(2 * ctx.shape[1] // FFT_N2)

    cc = jnp.concatenate([c, c_ctx[None], jnp.zeros((MOD_ROWS - batch - 1, d), F32)], axis=0)
    mod = _modulation(cc, ada_w, ada_b).reshape(DEPTH, MOD_ROWS, N_MOD, d)

    ffn_up_b = ffn_up.astype(BF16)
    ffn_down_b = ffn_down.astype(BF16)
    w_in_b = w_in.astype(BF16)
    gate_w_b = gate_w.astype(BF16)
    w_br_b = w_br.astype(BF16)
    w_o_b = w_o.astype(BF16)
    sg_w_b = sg_w.astype(BF16)

    xl, xc = x, ctx
    for l in range(DEPTH):
        last = l == DEPTH - 1
        mod_l = mod[l, :batch]
        mod_c = mod[l, batch:batch + 1]
        filt_params = (hy_f_w1[l], hy_f_b1[l], hy_f_w2[l], hy_f_b2[l], hy_f_freq[l], hy_f_w3[l])
        qk_gain_row = jnp.concatenate([jnp.tile(da_qk_gain[l, 0], 2 * DA_HEADS),
                                       jnp.tile(da_qk_gain[l, 1], 2 * DA_HEADS)]).reshape(1, -1)
        ws_stack = sg_w_b[l].reshape(SG_GROUPS * SG_CHUNK, SG_CHUNK)
        bs_tile = jnp.repeat(sg_b[l].T, SG_WIDTH // SG_GROUPS, axis=1)

        xl = _ffn(xl, mod_l, False, norm_g[l, 0], ffn_up_b[l, 0], ffn_down_b[l, 0], 0)
        xc = _ffn(xc, mod_c, True, norm_g[l, 0], ffn_up_b[l, 0], ffn_down_b[l, 0], 0)

        mix = functools.partial(_mixer_in, gain=norm_g[l, 1], w_in=w_in_b[l], qk_gain_row=qk_gain_row,
                                cos_t=cos_t, sin_t=sin_t, ones_bd=ones_bd, ln_g=sg_ln_g[l], ln_b=sg_ln_b[l],
                                ws_stack=ws_stack, bs_tile=bs_tile)
        ql, kl, vl, hy_l, yc_l = mix(xl, mod_l, False, rope=True)
        qc, kc, vc, hy_c, yc_c = mix(xc, mod_c, True, rope=False)

        k_all = jnp.concatenate([kc, kl], axis=1)
        v_all = jnp.concatenate([vc, vl], axis=1)
        ya_l = _attention(ql, k_all, v_all, da_lambda[l], da_subln[l], l)
        yb_l = _hyena(hy_l, hy_conv_w[l], hy_conv_b[l], filt_params, hy_skip[l], tables_l)
        merge = functools.partial(_merge, gain=norm_g[l, 1], gate_w=gate_w_b[l], gate_b=gate_b[l],
                                  w_br=w_br_b[l], w_o=w_o_b[l])
        xl = merge(xl, mod_l, False, ya=ya_l, yb=yb_l, yc=yc_l)
        xl = _ffn(xl, mod_l, False, norm_g[l, 2], ffn_up_b[l, 1], ffn_down_b[l, 1], 2)

        if not last:
            ya_c = _attention(qc, kc, vc, da_lambda[l], da_subln[l], l)
            yb_c = _hyena(hy_c, hy_conv_w[l], hy_conv_b[l], filt_params, hy_skip[l], tables_c)
            xc = merge(xc, mod_c, True, ya=ya_c, yb=yb_c, yc=yc_c)
            xc = _ffn(xc, mod_c, True, norm_g[l, 2], ffn_up_b[l, 1], ffn_down_b[l, 1], 2)
    return xl
```

```python
import functools
import math

import numpy as np
import jax
import jax.numpy as jnp
from jax import lax
from jax.experimental import pallas as pl
from jax.experimental.pallas import tpu as pltpu

F32 = jnp.float32
BF16 = jnp.bfloat16

D_MODEL = 1024
DEPTH = 2
GRID_W = 64
EPS = 1e-6
N_MOD = 9
FFN_HIDDEN = 2816

DA_HEADS = 4
DA_DH = 64
DA_DV = 2 * DA_DH
DA_WIDTH = DA_HEADS * DA_DV
ROPE_BASE = 10000.0
ROPE_PAIRS = DA_DH // 4

HY_WIDTH = 256
HY_ORDER = 2
HY_EMB = 33
HY_BANDS = (HY_EMB - 1) // 2
HY_FAST_DECAY = 0.3
HY_SLOW_DECAY = 1.5
HY_TARGET = 1e-2

SG_WIDTH = 256
SG_GROUPS = 4
SG_CHUNK = 128

HY_OFF = 3 * DA_WIDTH
SG_OFF = HY_OFF + (HY_ORDER + 1) * HY_WIDTH
IN_WIDTH = SG_OFF + 2 * SG_WIDTH

VMEM_LIMIT_BYTES = 56 * 1024 * 1024
LONG_CONV_VMEM_BYTES = 60 * 1024 * 1024
LANES = 128
SUBLANES = 8
FFT_N2 = 128
MOD_ROWS = 16
Q_SCALE = math.log2(math.e) * DA_DH ** -0.5
ATTN_Q_TILE = 512
ATTN_KEY_BLOCK = 256
ATTN_M_INIT = -1e30


def _cparams(*sem, vmem=VMEM_LIMIT_BYTES):
    return pltpu.CompilerParams(dimension_semantics=sem, vmem_limit_bytes=vmem)


def _resident(shape, index_map):
    return pl.BlockSpec(shape, index_map, pipeline_mode=pl.Buffered(1))


def _dot(a, b):
    return jnp.dot(a, b, preferred_element_type=F32)


def _split(a):
    hi = a.astype(BF16)
    lo = (a - hi.astype(F32)).astype(BF16)
    return hi, lo


def _dot3(a, b):
    ah, al = _split(a)
    bh, bl = _split(b)
    return _dot(ah, bh) + (_dot(al, bh) + _dot(ah, bl))


def _sigmoid(x):
    return 1.0 / (1.0 + jnp.exp(-x))


def _adaln(x, gain, shift, scale):
    y = x * lax.rsqrt(jnp.mean(x * x, axis=-1, keepdims=True) + EPS)
    return (y * gain) * (1.0 + scale) + shift


def _mod_kernel(c_ref, w_ref, b_ref, o_ref):
    c = c_ref[...]
    s = c * _sigmoid(c)
    o_ref[0] = _dot(s.astype(BF16), w_ref[0].astype(BF16)) + b_ref[0]


def _modulation(cc, ada_w, ada_b):
    depth, d, n = ada_w.shape
    tn = 1536
    return pl.pallas_call(
        _mod_kernel,
        grid=(depth, n // tn),
        in_specs=[
            pl.BlockSpec((MOD_ROWS, d), lambda l, j: (0, 0)),
            pl.BlockSpec((1, d, tn), lambda l, j: (l, 0, j)),
            pl.BlockSpec((1, 1, tn), lambda l, j: (l, 0, j)),
        ],
        out_specs=pl.BlockSpec((1, MOD_ROWS, tn), lambda l, j: (l, 0, j)),
        out_shape=jax.ShapeDtypeStruct((depth, MOD_ROWS, n), F32),
        compiler_params=_cparams("parallel", "parallel"),
        name="modulation",
    )(cc, ada_w, ada_b.reshape(depth, 1, n))


def _ffn_kernel(x_ref, mod_ref, g_ref, wu_ref, wd_ref, o_ref, *, sub):
    x = x_ref[0]
    shift = mod_ref[0, 3 * sub:3 * sub + 1, :]
    scale = mod_ref[0, 3 * sub + 1:3 * sub + 2, :]
    gate = mod_ref[0, 3 * sub + 2:3 * sub + 3, :]
    h = _adaln(x, g_ref[...], shift, scale).astype(BF16)
    a = _dot(h, wu_ref[:, :FFN_HIDDEN])
    b = _dot(h, wu_ref[:, FFN_HIDDEN:])
    g = (a * _sigmoid(a) * b).astype(BF16)
    o_ref[0] = x + (0.5 * gate) * _dot(g, wd_ref[...])


def _ffn(x, mod, shared_mod, gain, w_up, w_down, sub):
    b, s, d = x.shape
    tm = min(s, 512)
    mod_map = (lambda i, j: (0, 0, 0)) if shared_mod else (lambda i, j: (i, 0, 0))
    return pl.pallas_call(
        functools.partial(_ffn_kernel, sub=sub),
        grid=(b, s // tm),
        in_specs=[
            pl.BlockSpec((1, tm, d), lambda i, j: (i, j, 0)),
            pl.BlockSpec((1, N_MOD, d), mod_map),
            pl.BlockSpec((1, d), lambda i, j: (0, 0)),
            _resident((d, 2 * FFN_HIDDEN), lambda i, j: (0, 0)),
            _resident((FFN_HIDDEN, d), lambda i, j: (0, 0)),
        ],
        out_specs=pl.BlockSpec((1, tm, d), lambda i, j: (i, j, 0)),
        out_shape=jax.ShapeDtypeStruct((b, s, d), F32),
        compiler_params=_cparams("parallel", "parallel"),
        name=f"ffn{sub}",
    )(x, mod, gain.reshape(1, d), w_up, w_down)


def _group_sums(sq, ones_bd):
    outs = []
    for j in range(sq.shape[1] // 256):
        hi, lo = _split(sq[:, 256 * j:256 * (j + 1)])
        outs.append(_dot(hi, ones_bd) + _dot(lo, ones_bd))
    return jnp.concatenate(outs, axis=-1)


def _swap16(x):
    n = x.shape[-1]
    lane = lax.broadcasted_iota(jnp.int32, x.shape, x.ndim - 1)
    up = pltpu.roll(x, 16, x.ndim - 1)
    dn = pltpu.roll(x, n - 16, x.ndim - 1)
    return jnp.where((lane & 16) != 0, up, dn)


def _mixer_in_kernel(x_ref, mod_ref, g_ref, w_ref, qkg_ref, cos_ref, sin_ref, ones_ref,
                     lng_ref, lnb_ref, ws_ref, bs_ref,
                     q_ref, k_ref, v_ref, hy_ref, yc_ref, *, rope):
    x = x_ref[0]
    tm = x.shape[0]
    h = _adaln(x, g_ref[...], mod_ref[0, 3:4, :], mod_ref[0, 4:5, :]).astype(BF16)

    qk = _dot(h, w_ref[:, :2 * DA_WIDTH])
    ss = _group_sums(qk * qk, ones_ref[...])
    qk = qk * lax.rsqrt(ss * (1.0 / DA_DH) + EPS) * qkg_ref[...]
    if rope:
        reps = 2 * DA_WIDTH // LANES
        cosf = jnp.concatenate([cos_ref[...]] * reps, axis=-1)
        sinf = jnp.concatenate([sin_ref[...]] * reps, axis=-1)
        qk = qk * cosf + _swap16(qk) * sinf
    q_ref[0] = (qk[:, :DA_WIDTH] * Q_SCALE).astype(BF16)
    k_ref[0] = qk[:, DA_WIDTH:].astype(BF16)

    v_ref[0] = _dot(h, w_ref[:, 2 * DA_WIDTH:HY_OFF]).astype(BF16)
    hy_ref[0] = _dot(h, w_ref[:, HY_OFF:SG_OFF])

    sg = _dot(h, w_ref[:, SG_OFF:])
    sg = 0.5 * sg * (1.0 + lax.erf(sg * (2.0 ** -0.5)))
    u = sg[:, :SG_WIDTH]
    vv = sg[:, SG_WIDTH:]
    mu = jnp.mean(vv, axis=-1, keepdims=True)
    var = jnp.mean(jnp.square(vv - mu), axis=-1, keepdims=True)
    vv = ((vv - mu) * lax.rsqrt(var + EPS) * lng_ref[...] + lnb_ref[...]).astype(BF16)
    group = lax.broadcasted_iota(jnp.int32, (SG_CHUNK, SG_WIDTH), 1) // (SG_WIDTH // SG_GROUPS)
    for c in range(tm // SG_CHUNK):
        rows = slice(c * SG_CHUNK, (c + 1) * SG_CHUNK)
        mixed = _dot(ws_ref[...], vv[rows])
        sel = mixed[:SG_CHUNK]
        for g in range(1, SG_GROUPS):
            sel = jnp.where(group == g, mixed[g * SG_CHUNK:(g + 1) * SG_CHUNK], sel)
        yc_ref[0, rows, :] = (u[rows] * (sel + bs_ref[...])).astype(BF16)


def _mixer_in(x, mod, shared_mod, gain, w_in, qk_gain_row, cos_t, sin_t, ones_bd,
              ln_g, ln_b, ws_stack, bs_tile, rope):
    b, s, d = x.shape
    tm = min(s, 512)
    mod_map = (lambda i, j: (0, 0, 0)) if shared_mod else (lambda i, j: (i, 0, 0))
    const = lambda i, j: (0, 0)
    tok = lambda i, j: (i, j, 0)
    hy_w = SG_OFF - HY_OFF
    return pl.pallas_call(
        functools.partial(_mixer_in_kernel, rope=rope),
        grid=(b, s // tm),
        in_specs=[
            pl.BlockSpec((1, tm, d), tok),
            pl.BlockSpec((1, N_MOD, d), mod_map),
            pl.BlockSpec((1, d), const),
            _resident((d, IN_WIDTH), const),
            pl.BlockSpec((1, 2 * DA_WIDTH), const),
            pl.BlockSpec((tm, LANES), lambda i, j: (j, 0)),
            pl.BlockSpec((tm, LANES), lambda i, j: (j, 0)),
            pl.BlockSpec((256, 256), const),
            pl.BlockSpec((1, SG_WIDTH), const),
            pl.BlockSpec((1, SG_WIDTH), const),
            pl.BlockSpec((SG_GROUPS * SG_CHUNK, SG_CHUNK), const),
            pl.BlockSpec((SG_CHUNK, SG_WIDTH), const),
        ],
        out_specs=[
            pl.BlockSpec((1, tm, DA_WIDTH), tok),
            pl.BlockSpec((1, tm, DA_WIDTH), tok),
            pl.BlockSpec((1, tm, DA_WIDTH), tok),
            pl.BlockSpec((1, tm, hy_w), tok),
            pl.BlockSpec((1, tm, SG_WIDTH), tok),
        ],
        out_shape=[
            jax.ShapeDtypeStruct((b, s, DA_WIDTH), BF16),
            jax.ShapeDtypeStruct((b, s, DA_WIDTH), BF16),
            jax.ShapeDtypeStruct((b, s, DA_WIDTH), BF16),
            jax.ShapeDtypeStruct((b, s, hy_w), F32),
            jax.ShapeDtypeStruct((b, s, SG_WIDTH), BF16),
        ],
        compiler_params=_cparams("parallel", "parallel"),
        name="mixer_in_rope" if rope else "mixer_in",
    )(x, mod, gain.reshape(1, d), w_in, qk_gain_row, cos_t, sin_t, ones_bd,
      ln_g.reshape(1, -1), ln_b.reshape(1, -1), ws_stack, bs_tile)


def _attn_kernel(q_ref, k_ref, vt_ref, lv_ref, sub_ref, o_ref, s_ref, acc_ref, *, lam_init):
    lv = lv_ref[...]
    lam = (jnp.exp(jnp.sum(lv[0:1] * lv[1:2], axis=-1, keepdims=True))
           - jnp.exp(jnp.sum(lv[2:3] * lv[3:4], axis=-1, keepdims=True)) + lam_init)
    q = q_ref[0]
    tq = q.shape[0]
    nblk = k_ref.shape[1] // ATTN_KEY_BLOCK
    assert nblk % 2 == 1
    nt = (((1,), (1,)), ((), ()))

    def scores(start, slot):
        kblk = k_ref[0, pl.ds(start, ATTN_KEY_BLOCK), :]
        for c in range(2):
            cols = slice(c * DA_DH, (c + 1) * DA_DH)
            s_ref[slot, c] = lax.dot_general(kblk[:, cols], q[:, cols], nt, preferred_element_type=F32)

    def absorb(start, slot, stats):
        vblk = vt_ref[0, :, pl.ds(start, ATTN_KEY_BLOCK)]
        out = []
        for c in range(2):
            m, l = stats[c]
            s = s_ref[slot, c]
            m_new = jnp.maximum(m, jnp.max(s, axis=0, keepdims=True))
            alpha = jnp.exp2(m - m_new)
            p = jnp.exp2(s - m_new)
            acc_ref[c] = alpha * acc_ref[c] + _dot(vblk, p.astype(BF16))
            out.append((m_new, alpha * l + jnp.sum(p, axis=0, keepdims=True)))
        return tuple(out)

    acc_ref[...] = jnp.zeros(acc_ref.shape, F32)
    stats = tuple((jnp.full((1, tq), ATTN_M_INIT, F32), jnp.zeros((1, tq), F32)) for _ in range(2))
    scores(0, 0)

    def pair(i, stats):
        first, second, third = (pl.multiple_of((2 * i + d) * ATTN_KEY_BLOCK, ATTN_KEY_BLOCK) for d in range(3))
        scores(second, 1)
        stats = absorb(first, 0, stats)
        scores(third, 0)
        return absorb(second, 1, stats)

    stats = lax.fori_loop(0, nblk // 2, pair, stats)
    (_, l0), (_, l1) = absorb((nblk - 1) * ATTN_KEY_BLOCK, 0, stats)
    o = acc_ref[0] * (1.0 / l0) - acc_ref[1] * (lam / l1)
    y = o * lax.rsqrt(jnp.mean(o * o, axis=0, keepdims=True) + EPS)
    sub = jnp.concatenate([sub_ref[...]] * (tq // LANES), axis=-1)
    o_ref[0] = (y * sub * (1.0 - lam_init)).T.astype(BF16)


def _attention(q, k, v_t, lam_vecs, subln, layer_idx):
    b, sq, _ = q.shape
    sk = k.shape[1]
    tq = min(sq, ATTN_Q_TILE)
    lam_init = 0.8 - 0.6 * math.exp(-0.3 * layer_idx)
    return pl.pallas_call(
        functools.partial(_attn_kernel, lam_init=lam_init),
        grid=(b, DA_HEADS, sq // tq),
        in_specs=[
            pl.BlockSpec((1, tq, DA_DV), lambda i, h, j: (i, j, h)),
            pl.BlockSpec((1, sk, DA_DV), lambda i, h, j: (i, 0, h)),
            pl.BlockSpec((1, DA_DV, sk), lambda i, h, j: (i, h, 0)),
            pl.BlockSpec((4, DA_DH), lambda i, h, j: (0, 0)),
            pl.BlockSpec((DA_DV, LANES), lambda i, h, j: (0, 0)),
        ],
        out_specs=pl.BlockSpec((1, tq, DA_DV), lambda i, h, j: (i, j, h)),
        out_shape=jax.ShapeDtypeStruct((b, sq, DA_WIDTH), BF16),
        scratch_shapes=[pltpu.VMEM((2, 2, min(sk, ATTN_KEY_BLOCK), tq), F32),
                        pltpu.VMEM((2, DA_DV, tq), F32)],
        compiler_params=_cparams("parallel", "parallel", "parallel"),
        name="diff_attention",
    )(q, k, v_t, lam_vecs, jnp.broadcast_to(subln[:, None], (DA_DV, LANES)))


def _short_conv_kernel(p0_ref, p1_ref, p2_ref, w0_ref, w1_ref, w2_ref, b0_ref, b1_ref, b2_ref,
                       o0_ref, o1_ref, o2_ref):
    n = p0_ref.shape[1]
    row = lax.broadcasted_iota(jnp.int32, (n, LANES), 0)
    for p_ref, w_ref, b_ref, o_ref in ((p0_ref, w0_ref, b0_ref, o0_ref),
                                       (p1_ref, w1_ref, b1_ref, o1_ref),
                                       (p2_ref, w2_ref, b2_ref, o2_ref)):
        p = p_ref[0]
        prev = jnp.where(row == 0, 0.0, pltpu.roll(p, 1, 0))
        nxt = jnp.where(row == n - 1, 0.0, pltpu.roll(p, n - 1, 0))
        w = w_ref[...]
        o_ref[0] = prev * w[0:1] + p * w[1:2] + nxt * w[2:3] + b_ref[...]


def _short_conv(p, conv_w, conv_b):
    b, n, _ = p.shape
    halves = HY_WIDTH // LANES
    seq = lambda part: pl.BlockSpec((1, n, LANES), lambda i, j: (i, 0, part * halves + j))
    wsp = lambda part: pl.BlockSpec((3, LANES), lambda i, j: (0, part * halves + j))
    bsp = lambda part: pl.BlockSpec((1, LANES), lambda i, j: (0, part * halves + j))
    out = pl.BlockSpec((1, n, LANES), lambda i, j: (i, 0, j))
    shp = jax.ShapeDtypeStruct((b, n, HY_WIDTH), F32)
    cb = conv_b.reshape(1, -1)
    return pl.pallas_call(
        _short_conv_kernel,
        grid=(b, halves),
        in_specs=[seq(0), seq(1), seq(2), wsp(0), wsp(1), wsp(2), bsp(0), bsp(1), bsp(2)],
        out_specs=[out, out, out],
        out_shape=[shp, shp, shp],
        compiler_params=_cparams("parallel", "parallel"),
        name="hyena_short_conv",
    )(p, p, p, conv_w, conv_w, conv_w, cb, cb, cb)


def _filter_kernel(z_ref, dec_ref, w1_ref, b1_ref, w2_ref, b2_ref, fr_ref, w3_ref, f_ref, l1_ref):
    i = pl.program_id(0)
    fr = fr_ref[...]
    h = jnp.sin(fr * (_dot3(z_ref[...], w1_ref[...]) + b1_ref[...]))
    h = jnp.sin(fr * (_dot3(h, w2_ref[...]) + b2_ref[...]))
    h = _dot3(h, w3_ref[...])
    tr = h.shape[0]
    half = HY_ORDER * HY_WIDTH
    dec = dec_ref[...]
    fwd = h[:, :half] * dec
    first = (lax.broadcasted_iota(jnp.int32, (tr, half), 0) + i * tr) == 0
    bwd = jnp.where(first, 0.0, h[:, half:] * dec)
    for o in range(HY_ORDER):
        f_ref[o] = fwd[:, o * HY_WIDTH:(o + 1) * HY_WIDTH]
        f_ref[HY_ORDER + o] = bwd[:, o * HY_WIDTH:(o + 1) * HY_WIDTH]
    part = jnp.sum(jnp.abs(fwd) + jnp.abs(bwd), axis=0, keepdims=True)

    @pl.when(i == 0)
    def _():
        l1_ref[...] = part

    @pl.when(i != 0)
    def _():
        l1_ref[...] += part


def _hyena_filter_taps(n, w1, b1, w2, b2, freq, w3):
    t = jnp.linspace(0.0, 1.0, n, dtype=F32)[:, None]
    w = (2.0 * math.pi / n) * jnp.arange(n, dtype=F32)[:, None]
    bands = jnp.linspace(1e-4, HY_BANDS - 1, HY_BANDS, dtype=F32)
    z = jnp.concatenate([t, jnp.cos(bands * w), -jnp.sin(bands * w)], axis=-1)
    emb = 48
    z = jnp.pad(z, ((0, 0), (0, emb - HY_EMB)))
    w1p = jnp.pad(w1, ((0, emb - HY_EMB), (0, 0)))
    max_decay = math.log(HY_TARGET) / HY_FAST_DECAY
    min_decay = math.log(HY_TARGET) / HY_SLOW_DECAY
    deltas = jnp.linspace(min_decay, max_decay, HY_ORDER * HY_WIDTH, dtype=F32)
    decay = jnp.exp(-t * jnp.abs(deltas)[None, :])
    tr = min(n, 512)
    hid = w2.shape[0]
    half = HY_ORDER * HY_WIDTH
    const = lambda i: (0, 0)
    return pl.pallas_call(
        _filter_kernel,
        grid=(n // tr,),
        in_specs=[
            pl.BlockSpec((tr, emb), lambda i: (i, 0)),
            pl.BlockSpec((tr, half), lambda i: (i, 0)),
            pl.BlockSpec((emb, hid), const),
            pl.BlockSpec((1, hid), const),
            pl.BlockSpec((hid, hid), const),
            pl.BlockSpec((1, hid), const),
            pl.BlockSpec((1, hid), const),
            pl.BlockSpec((hid, 2 * half), const),
        ],
        out_specs=[
            pl.BlockSpec((2 * HY_ORDER, tr, HY_WIDTH), lambda i: (0, i, 0)),
            pl.BlockSpec((1, half), const),
        ],
        out_shape=[
            jax.ShapeDtypeStruct((2 * HY_ORDER, n, HY_WIDTH), F32),
            jax.ShapeDtypeStruct((1, half), F32),
        ],
        compiler_params=_cparams("arbitrary"),
        name="hyena_filter_taps",
    )(z, decay, w1p, b1.reshape(1, -1), w2, b2.reshape(1, -1), freq.reshape(1, -1), w3)


@functools.lru_cache(maxsize=None)
def _dft_tables(n1):
    n2 = FFT_N2
    n = n1 * n2
    half = n1 // 2
    k1 = np.arange(half + 1)[:, None].astype(np.float64)
    t1 = np.arange(half)[None, :].astype(np.float64)
    ang_a = 2.0 * np.pi * k1 * t1 / n1
    eye = np.eye(SUBLANES)
    wa_fwd = np.kron(np.concatenate([np.cos(ang_a), -np.sin(ang_a)], axis=0), eye)
    weight = np.where(k1[:half] == 0, 1.0, 2.0)
    wa_inv = np.kron(np.concatenate([(weight * np.cos(ang_a[:half])).T,
                                     (-weight * np.sin(ang_a[:half])).T], axis=1), eye)
    kk1 = np.arange(half + 1)[:, None, None].astype(np.float64)
    kk2 = np.arange(n2)[None, :, None].astype(np.float64)
    tt2 = np.arange(n2)[None, None, :].astype(np.float64)
    ang_b = 2.0 * np.pi * tt2 * (kk1 + n1 * kk2) / n
    er, ei = np.cos(ang_b), -np.sin(ang_b)
    mf = np.concatenate([np.concatenate([er, -ei], axis=2),
                         np.concatenate([ei, er], axis=2)], axis=1)
    mi = np.swapaxes(mf, 1, 2)
    f32 = lambda a: np.ascontiguousarray(a.astype(np.float32))
    return f32(wa_fwd), f32(wa_inv), f32(mf), f32(mi)


def _tile_rows(ref, g, count):
    start = pl.multiple_of(g * SUBLANES, SUBLANES)
    return jnp.concatenate([ref[pl.ds(t * FFT_N2 + start, SUBLANES), :] for t in range(count)], axis=0)


def _dft_forward(u_ref, a_ref, wa_ref, mf_ref, half, emit):
    ks = half + 1

    def stage_a(g, carry):
        start = pl.multiple_of(g * SUBLANES, SUBLANES)
        r = _dot(wa_ref[...], _tile_rows(u_ref, g, half).astype(BF16))
        for part in range(2):
            for k1 in range(ks):
                src = (part * ks + k1) * SUBLANES
                a_ref[part, pl.ds(k1 * FFT_N2 + start, SUBLANES), :] = r[src:src + SUBLANES]
        return carry

    lax.fori_loop(0, FFT_N2 // SUBLANES, stage_a, 0, unroll=2)

    group = 3 if ks % 3 == 0 else 1

    def stage_b(i, carry):
        base = pl.multiple_of(i * (group * FFT_N2), FFT_N2)
        rows = [pl.ds(base + j * FFT_N2, FFT_N2) for j in range(group)]
        xs = [_dot(mf_ref[i * group + j],
                   jnp.concatenate([a_ref[0, rows[j], :], a_ref[1, rows[j], :]], axis=0).astype(BF16))
              for j in range(group)]
        emit(i * group, rows, xs)
        return carry

    lax.fori_loop(0, ks // group, stage_b, 0)


def _spectrum_kernel(t_ref, wa_ref, mf_ref, l1_ref, h_ref, a_ref, *, half, scale):
    d = pl.program_id(1)
    inv = scale / l1_ref[0]

    def emit(k0, rows, xs):
        @pl.when(d == 0)
        def _():
            for j, x in enumerate(xs):
                h_ref[0, 0, k0 + j] = x[:FFT_N2] * inv
                h_ref[0, 1, k0 + j] = x[FFT_N2:] * inv

        @pl.when(d != 0)
        def _():
            for j, x in enumerate(xs):
                h_ref[0, 0, k0 + j] += x[:FFT_N2] * inv
                h_ref[0, 1, k0 + j] -= x[FFT_N2:] * inv

    _dft_forward(t_ref, a_ref, wa_ref, mf_ref, half, emit)


def _filter_spectrum(taps, wa_fwd, mf, l1, n1):
    _, n, c = taps.shape
    half = n1 // 2
    ks = half + 1
    return pl.pallas_call(
        functools.partial(_spectrum_kernel, half=half, scale=1.0 / (n1 * FFT_N2)),
        grid=(HY_ORDER, 2),
        in_specs=[pl.BlockSpec((None, n, c), lambda o, d: (d * HY_ORDER + o, 0, 0)),
                  pl.BlockSpec(wa_fwd.shape, lambda o, d: (0, 0)),
                  _resident(mf.shape, lambda o, d: (0, 0, 0)),
                  pl.BlockSpec((1, 1, c), lambda o, d: (o, 0, 0))],
        out_specs=pl.BlockSpec((1, 2, ks, FFT_N2, c), lambda o, d: (o, 0, 0, 0, 0)),
        out_shape=jax.ShapeDtypeStruct((HY_ORDER, 2, ks, FFT_N2, c), F32),
        scratch_shapes=[pltpu.VMEM((2, ks * FFT_N2, c), F32)],
        compiler_params=_cparams("parallel", "arbitrary"),
        name="hyena_filter_spectrum",
    )(taps, wa_fwd, mf, l1.reshape(HY_ORDER, 1, c))


def _long_conv_kernel(u_ref, g_ref, skip_ref, wa_ref, wi_ref, mf_ref, mi_ref, h_ref, o_ref, a_ref, *, half):
    u_ref, g_ref, o_ref = u_ref.at[0], g_ref.at[0], o_ref.at[0]

    def emit(k0, rows, xs):
        cs = []
        for j, x in enumerate(xs):
            xr, xi = x[:FFT_N2], x[FFT_N2:]
            hr, hi = h_ref[0, k0 + j], h_ref[1, k0 + j]
            y = jnp.concatenate([xr * hr - xi * hi, xr * hi + xi * hr], axis=0).astype(BF16)
            cs.append(_dot(mi_ref[k0 + j], y))
        for j, c in enumerate(cs):
            a_ref[0, rows[j], :] = c[:FFT_N2]
            a_ref[1, rows[j], :] = c[FFT_N2:]

    _dft_forward(u_ref, a_ref, wa_ref, mf_ref, half, emit)

    tile_rows = half * SUBLANES
    alt = (1 - 2 * ((lax.broadcasted_iota(jnp.int32, (tile_rows, 1), 0) // SUBLANES) & 1)).astype(F32)
    skip = skip_ref[...]

    def stage_a_inv(g, carry):
        start = pl.multiple_of(g * SUBLANES, SUBLANES)
        c = jnp.concatenate([_tile_rows(a_ref.at[0], g, half), _tile_rows(a_ref.at[1], g, half)], axis=0)
        nyq = a_ref[0, pl.ds(half * FFT_N2 + start, SUBLANES), :]
        y = _dot(wi_ref[...], c.astype(BF16)) + alt * jnp.concatenate([nyq] * half, axis=0)
        out = _tile_rows(g_ref, g, half) * (y + _tile_rows(u_ref, g, half) * skip)
        for t1 in range(half):
            o_ref[pl.ds(t1 * FFT_N2 + start, SUBLANES), :] = out[t1 * SUBLANES:(t1 + 1) * SUBLANES]
        return carry

    lax.fori_loop(0, FFT_N2 // SUBLANES, stage_a_inv, 0, unroll=2)


def _long_conv(u, gate, skip_row, tables, hspec, order, n1):
    b, n, c = u.shape
    half = n1 // 2
    ks = half + 1
    wa_fwd, wa_inv, mf, mi = tables
    seq = pl.BlockSpec((1, n, c), lambda i: (i, 0, 0))
    const2 = lambda i: (0, 0)
    const3 = lambda i: (0, 0, 0)
    return pl.pallas_call(
        functools.partial(_long_conv_kernel, half=half),
        grid=(b,),
        in_specs=[seq, seq,
                  pl.BlockSpec((1, c), const2),
                  pl.BlockSpec(wa_fwd.shape, const2),
                  pl.BlockSpec(wa_inv.shape, const2),
                  _resident(mf.shape, const3),
                  _resident(mi.shape, const3),
                  _resident((None, 2, ks, FFT_N2, c), lambda i: (order, 0, 0, 0, 0))],
        out_specs=seq,
        out_shape=jax.ShapeDtypeStruct(u.shape, F32),
        scratch_shapes=[pltpu.VMEM((2, ks * FFT_N2, c), F32)],
        compiler_params=_cparams("parallel", vmem=LONG_CONV_VMEM_BYTES),
        name="hyena_long_conv",
    )(u, gate, skip_row, wa_fwd, wa_inv, mf, mi, hspec)


def _hyena(p, conv_w, conv_b, filt_params, skip, tables):
    b, n, _ = p.shape
    n1 = 2 * n // FFT_N2
    wa_fwd, _, mf, _ = tables
    v, x1, x2 = _short_conv(p, conv_w, conv_b)
    taps, l1 = _hyena_filter_taps(n, *filt_params)
    hspec = _filter_spectrum(taps, wa_fwd, mf, l1, n1)
    z = _long_conv(v, x1, skip[0:1], tables, hspec, 0, n1)
    return _long_conv(z, x2, skip[1:2], tables, hspec, 1, n1)


def _merge_kernel(x_ref, mod_ref, g_ref, ya_ref, yb_ref, yc_ref, gw_ref, gb_ref, wbr_ref, wo_ref, o_ref):
    x = x_ref[0]
    d = x.shape[1]
    h = _adaln(x, g_ref[...], mod_ref[0, 3:4, :], mod_ref[0, 4:5, :]).astype(BF16)
    gates = _sigmoid(_dot(h, gw_ref[...]) + gb_ref[...])
    m = (gates[:, :d] * _dot(ya_ref[0], wbr_ref[:DA_WIDTH])
         + gates[:, d:2 * d] * _dot(yb_ref[0].astype(BF16), wbr_ref[DA_WIDTH:DA_WIDTH + HY_WIDTH])
         + gates[:, 2 * d:] * _dot(yc_ref[0], wbr_ref[DA_WIDTH + HY_WIDTH:]))
    o_ref[0] = x + mod_ref[0, 5:6, :] * _dot(m.astype(BF16), wo_ref[...])


def _merge(x, mod, shared_mod, gain, ya, yb, yc, gate_w, gate_b, w_br, w_o):
    b, s, d = x.shape
    tm = min(s, 512)
    mod_map = (lambda i, j: (0, 0, 0)) if shared_mod else (lambda i, j: (i, 0, 0))
    const = lambda i, j: (0, 0)
    tok = lambda i, j: (i, j, 0)
    return pl.pallas_call(
        _merge_kernel,
        grid=(b, s // tm),
        in_specs=[
            pl.BlockSpec((1, tm, d), tok),
            pl.BlockSpec((1, N_MOD, d), mod_map),
            pl.BlockSpec((1, d), const),
            pl.BlockSpec((1, tm, DA_WIDTH), tok),
            pl.BlockSpec((1, tm, HY_WIDTH), tok),
            pl.BlockSpec((1, tm, SG_WIDTH), tok),
            _resident(gate_w.shape, const),
            pl.BlockSpec((1, gate_w.shape[1]), const),
            _resident(w_br.shape, const),
            _resident(w_o.shape, const),
        ],
        out_specs=pl.BlockSpec((1, tm, d), tok),
        out_shape=jax.ShapeDtypeStruct((b, s, d), F32),
        compiler_params=_cparams("parallel", "parallel"),
        name="branch_merge",
    )(x, mod, gain.reshape(1, d), ya, yb, yc, gate_w, gate_b.reshape(1, -1), w_br, w_o)


def _rope_tables(n_tokens):
    t = np.arange(n_tokens)
    pos = np.stack([t // GRID_W, t % GRID_W], axis=-1).astype(np.float64)
    inv = ROPE_BASE ** (-np.arange(ROPE_PAIRS, dtype=np.float64) / ROPE_PAIRS)
    ang = pos[:, :, None] * inv
    cos, sin = np.cos(ang), np.sin(ang)
    cos64 = np.concatenate([cos[:, 0], cos[:, 0], cos[:, 1], cos[:, 1]], axis=-1)
    sin64 = np.concatenate([-sin[:, 0], sin[:, 0], -sin[:, 1], sin[:, 1]], axis=-1)
    as_f32 = lambda a: np.tile(a, (1, LANES // DA_DH)).astype(np.float32)
    return as_f32(cos64), as_f32(sin64)


def _device_tables(n1):
    return tuple(jnp.asarray(t).astype(BF16) for t in _dft_tables(n1))


def kernel(x, c, ctx, c_ctx, ada_w, ada_b, norm_g, ffn_up, ffn_down, w_in, da_qk_gain, da_lambda, da_subln, hy_conv_w, hy_conv_b, hy_f_w1, hy_f_b1, hy_f_w2, hy_f_b2, hy_f_freq, hy_f_w3, hy_skip, sg_ln_g, sg_ln_b, sg_w, sg_b, gate_w, gate_b, w_br, w_o):
    batch, seq, d = x.shape
    cos_t, sin_t = _rope_tables(seq)
    ones_bd = np.kron(np.eye(256 // DA_DH), np.ones((DA_DH, DA_DH))).astype(BF16)
    tables_l = _device_tables(2 * seq // FFT_N2)
    tables_c = _device_tables(2 * ctx.shape[1] // FFT_N2)

    cc = jnp.concatenate([c, c_ctx[None], jnp.zeros((MOD_ROWS - batch - 1, d), F32)], axis=0)
    mod = _modulation(cc, ada_w, ada_b).reshape(DEPTH, MOD_ROWS, N_MOD, d)

    ffn_up_b = ffn_up.astype(BF16)
    ffn_down_b = ffn_down.astype(BF16)
    w_in_b = w_in.astype(BF16)
    gate_w_b = gate_w.astype(BF16)
    w_br_b = w_br.astype(BF16)
    w_o_b = w_o.astype(BF16)
    sg_w_b = sg_w.astype(BF16)

    xl, xc = x, ctx
    for l in range(DEPTH):
        last = l == DEPTH - 1
        mod_l = mod[l, :batch]
        mod_c = mod[l, batch:batch + 1]
        filt_params = (hy_f_w1[l], hy_f_b1[l], hy_f_w2[l], hy_f_b2[l], hy_f_freq[l], hy_f_w3[l])
        qk_gain_row = jnp.concatenate([jnp.tile(da_qk_gain[l, 0], 2 * DA_HEADS),
                                       jnp.tile(da_qk_gain[l, 1], 2 * DA_HEADS)]).reshape(1, -1)
        ws_stack = sg_w_b[l].reshape(SG_GROUPS * SG_CHUNK, SG_CHUNK)
        bs_tile = jnp.repeat(sg_b[l].T, SG_WIDTH // SG_GROUPS, axis=1)

        xl = _ffn(xl, mod_l, False, norm_g[l, 0], ffn_up_b[l, 0], ffn_down_b[l, 0], 0)
        xc = _ffn(xc, mod_c, True, norm_g[l, 0], ffn_up_b[l, 0], ffn_down_b[l, 0], 0)

        mix = functools.partial(_mixer_in, gain=norm_g[l, 1], w_in=w_in_b[l], qk_gain_row=qk_gain_row,
                                cos_t=cos_t, sin_t=sin_t, ones_bd=ones_bd, ln_g=sg_ln_g[l], ln_b=sg_ln_b[l],
                                ws_stack=ws_stack, bs_tile=bs_tile)
        ql, kl, vl, hy_l, yc_l = mix(xl, mod_l, False, rope=True)
        qc, kc, vc, hy_c, yc_c = mix(xc, mod_c, True, rope=False)

        k_all = jnp.concatenate([kc, kl], axis=1)
        v_all = jnp.concatenate([vc, vl], axis=1)
        ya_l = _attention(ql, k_all, jnp.swapaxes(v_all, 1, 2), da_lambda[l], da_subln[l], l)
        yb_l = _hyena(hy_l, hy_conv_w[l], hy_conv_b[l], filt_params, hy_skip[l], tables_l)
        merge = functools.partial(_merge, gain=norm_g[l, 1], gate_w=gate_w_b[l], gate_b=gate_b[l],
                                  w_br=w_br_b[l], w_o=w_o_b[l])
        xl = merge(xl, mod_l, False, ya=ya_l, yb=yb_l, yc=yc_l)
        xl = _ffn(xl, mod_l, False, norm_g[l, 2], ffn_up_b[l, 1], ffn_down_b[l, 1], 2)

        if not last:
            ya_c = _attention(qc, kc, jnp.swapaxes(vc, 1, 2), da_lambda[l], da_subln[l], l)
            yb_c = _hyena(hy_c, hy_conv_w[l], hy_conv_b[l], filt_params, hy_skip[l], tables_c)
            xc = merge(xc, mod_c, True, ya=ya_c, yb=yb_c, yc=yc_c)
            xc = _ffn(xc, mod_c, True, norm_g[l, 2], ffn_up_b[l, 1], ffn_down_b[l, 1], 2)
    return xl
```

```python
import functools
import math

import numpy as np
import jax
import jax.numpy as jnp
from jax import lax
from jax.experimental import pallas as pl
from jax.experimental.pallas import tpu as pltpu

F32 = jnp.float32
BF16 = jnp.bfloat16

D_MODEL = 1024
DEPTH = 2
GRID_W = 64
EPS = 1e-6
N_MOD = 9
FFN_HIDDEN = 2816

DA_HEADS = 4
DA_DH = 64
DA_DV = 2 * DA_DH
DA_WIDTH = DA_HEADS * DA_DV
ROPE_BASE = 10000.0
ROPE_PAIRS = DA_DH // 4

HY_WIDTH = 256
HY_ORDER = 2
HY_EMB = 33
HY_BANDS = (HY_EMB - 1) // 2
HY_FAST_DECAY = 0.3
HY_SLOW_DECAY = 1.5
HY_TARGET = 1e-2

SG_WIDTH = 256
SG_GROUPS = 4
SG_CHUNK = 128

HY_OFF = 3 * DA_WIDTH
SG_OFF = HY_OFF + (HY_ORDER + 1) * HY_WIDTH
IN_WIDTH = SG_OFF + 2 * SG_WIDTH

VMEM_LIMIT_BYTES = 56 * 1024 * 1024
LONG_CONV_VMEM_BYTES = 60 * 1024 * 1024
LANES = 128
SUBLANES = 8
FFT_N2 = 128
MOD_ROWS = 16
Q_SCALE = math.log2(math.e) * DA_DH ** -0.5
ATTN_Q_TILE = 512
ATTN_KEY_BLOCK = 256
ATTN_VT_ROWS = DA_DV + 16
ATTN_BOUND_MARGIN = 1.02
ATTN_MIN_MASS = 2.0 ** -60
ATTN_M_INIT = -1e30


def _cparams(*sem, vmem=VMEM_LIMIT_BYTES):
    return pltpu.CompilerParams(dimension_semantics=sem, vmem_limit_bytes=vmem)


def _resident(shape, index_map):
    return pl.BlockSpec(shape, index_map, pipeline_mode=pl.Buffered(1))


def _dot(a, b):
    return jnp.dot(a, b, preferred_element_type=F32)


def _split(a):
    hi = a.astype(BF16)
    lo = (a - hi.astype(F32)).astype(BF16)
    return hi, lo


def _dot3(a, b):
    ah, al = _split(a)
    bh, bl = _split(b)
    return _dot(ah, bh) + (_dot(al, bh) + _dot(ah, bl))


def _sigmoid(x):
    return 1.0 / (1.0 + jnp.exp(-x))


def _adaln(x, gain, shift, scale):
    y = x * lax.rsqrt(jnp.mean(x * x, axis=-1, keepdims=True) + EPS)
    return (y * gain) * (1.0 + scale) + shift


def _mod_kernel(c_ref, w_ref, b_ref, o_ref):
    c = c_ref[...]
    s = c * _sigmoid(c)
    o_ref[0] = _dot(s.astype(BF16), w_ref[0].astype(BF16)) + b_ref[0]


def _modulation(cc, ada_w, ada_b):
    depth, d, n = ada_w.shape
    tn = 1536
    return pl.pallas_call(
        _mod_kernel,
        grid=(depth, n // tn),
        in_specs=[
            pl.BlockSpec((MOD_ROWS, d), lambda l, j: (0, 0)),
            pl.BlockSpec((1, d, tn), lambda l, j: (l, 0, j)),
            pl.BlockSpec((1, 1, tn), lambda l, j: (l, 0, j)),
        ],
        out_specs=pl.BlockSpec((1, MOD_ROWS, tn), lambda l, j: (l, 0, j)),
        out_shape=jax.ShapeDtypeStruct((depth, MOD_ROWS, n), F32),
        compiler_params=_cparams("parallel", "parallel"),
        name="modulation",
    )(cc, ada_w, ada_b.reshape(depth, 1, n))


def _ffn_kernel(x_ref, mod_ref, g_ref, wu_ref, wd_ref, o_ref, *, sub):
    x = x_ref[0]
    shift = mod_ref[0, 3 * sub:3 * sub + 1, :]
    scale = mod_ref[0, 3 * sub + 1:3 * sub + 2, :]
    gate = mod_ref[0, 3 * sub + 2:3 * sub + 3, :]
    h = _adaln(x, g_ref[...], shift, scale).astype(BF16)
    a = _dot(h, wu_ref[:, :FFN_HIDDEN])
    b = _dot(h, wu_ref[:, FFN_HIDDEN:])
    g = (a * _sigmoid(a) * b).astype(BF16)
    o_ref[0] = x + (0.5 * gate) * _dot(g, wd_ref[...])


def _ffn(x, mod, shared_mod, gain, w_up, w_down, sub):
    b, s, d = x.shape
    tm = min(s, 512)
    mod_map = (lambda i, j: (0, 0, 0)) if shared_mod else (lambda i, j: (i, 0, 0))
    return pl.pallas_call(
        functools.partial(_ffn_kernel, sub=sub),
        grid=(b, s // tm),
        in_specs=[
            pl.BlockSpec((1, tm, d), lambda i, j: (i, j, 0)),
            pl.BlockSpec((1, N_MOD, d), mod_map),
            pl.BlockSpec((1, d), lambda i, j: (0, 0)),
            _resident((d, 2 * FFN_HIDDEN), lambda i, j: (0, 0)),
            _resident((FFN_HIDDEN, d), lambda i, j: (0, 0)),
        ],
        out_specs=pl.BlockSpec((1, tm, d), lambda i, j: (i, j, 0)),
        out_shape=jax.ShapeDtypeStruct((b, s, d), F32),
        compiler_params=_cparams("parallel", "parallel"),
        name=f"ffn{sub}",
    )(x, mod, gain.reshape(1, d), w_up, w_down)


def _group_sums(sq, ones_bd):
    outs = []
    for j in range(sq.shape[1] // 256):
        hi, lo = _split(sq[:, 256 * j:256 * (j + 1)])
        outs.append(_dot(hi, ones_bd) + _dot(lo, ones_bd))
    return jnp.concatenate(outs, axis=-1)


def _swap16(x):
    n = x.shape[-1]
    lane = lax.broadcasted_iota(jnp.int32, x.shape, x.ndim - 1)
    up = pltpu.roll(x, 16, x.ndim - 1)
    dn = pltpu.roll(x, n - 16, x.ndim - 1)
    return jnp.where((lane & 16) != 0, up, dn)


def _mixer_in_kernel(x_ref, mod_ref, g_ref, w_ref, qkg_ref, cos_ref, sin_ref, ones_ref,
                     lng_ref, lnb_ref, ws_ref, bs_ref,
                     q_ref, k_ref, v_ref, hy_ref, yc_ref, *, rope):
    x = x_ref[0]
    tm = x.shape[0]
    h = _adaln(x, g_ref[...], mod_ref[0, 3:4, :], mod_ref[0, 4:5, :]).astype(BF16)

    qk = _dot(h, w_ref[:, :2 * DA_WIDTH])
    ss = _group_sums(qk * qk, ones_ref[...])
    qk = qk * lax.rsqrt(ss * (1.0 / DA_DH) + EPS) * qkg_ref[...]
    if rope:
        reps = 2 * DA_WIDTH // LANES
        cosf = jnp.concatenate([cos_ref[...]] * reps, axis=-1)
        sinf = jnp.concatenate([sin_ref[...]] * reps, axis=-1)
        qk = qk * cosf + _swap16(qk) * sinf
    q_ref[0] = (qk[:, :DA_WIDTH] * Q_SCALE).astype(BF16)
    k_ref[0] = qk[:, DA_WIDTH:].astype(BF16)

    v_ref[0] = _dot(h, w_ref[:, 2 * DA_WIDTH:HY_OFF]).astype(BF16)
    hy_ref[0] = _dot(h, w_ref[:, HY_OFF:SG_OFF])

    sg = _dot(h, w_ref[:, SG_OFF:])
    sg = 0.5 * sg * (1.0 + lax.erf(sg * (2.0 ** -0.5)))
    u = sg[:, :SG_WIDTH]
    vv = sg[:, SG_WIDTH:]
    mu = jnp.mean(vv, axis=-1, keepdims=True)
    var = jnp.mean(jnp.square(vv - mu), axis=-1, keepdims=True)
    vv = ((vv - mu) * lax.rsqrt(var + EPS) * lng_ref[...] + lnb_ref[...]).astype(BF16)
    group = lax.broadcasted_iota(jnp.int32, (SG_CHUNK, SG_WIDTH), 1) // (SG_WIDTH // SG_GROUPS)
    for c in range(tm // SG_CHUNK):
        rows = slice(c * SG_CHUNK, (c + 1) * SG_CHUNK)
        mixed = _dot(ws_ref[...], vv[rows])
        sel = mixed[:SG_CHUNK]
        for g in range(1, SG_GROUPS):
            sel = jnp.where(group == g, mixed[g * SG_CHUNK:(g + 1) * SG_CHUNK], sel)
        yc_ref[0, rows, :] = (u[rows] * (sel + bs_ref[...])).astype(BF16)


def _mixer_in(x, mod, shared_mod, gain, w_in, qk_gain_row, cos_t, sin_t, ones_bd,
              ln_g, ln_b, ws_stack, bs_tile, rope):
    b, s, d = x.shape
    tm = min(s, 512)
    mod_map = (lambda i, j: (0, 0, 0)) if shared_mod else (lambda i, j: (i, 0, 0))
    const = lambda i, j: (0, 0)
    tok = lambda i, j: (i, j, 0)
    hy_w = SG_OFF - HY_OFF
    return pl.pallas_call(
        functools.partial(_mixer_in_kernel, rope=rope),
        grid=(b, s // tm),
        in_specs=[
            pl.BlockSpec((1, tm, d), tok),
            pl.BlockSpec((1, N_MOD, d), mod_map),
            pl.BlockSpec((1, d), const),
            _resident((d, IN_WIDTH), const),
            pl.BlockSpec((1, 2 * DA_WIDTH), const),
            pl.BlockSpec((tm, LANES), lambda i, j: (j, 0)),
            pl.BlockSpec((tm, LANES), lambda i, j: (j, 0)),
            pl.BlockSpec((256, 256), const),
            pl.BlockSpec((1, SG_WIDTH), const),
            pl.BlockSpec((1, SG_WIDTH), const),
            pl.BlockSpec((SG_GROUPS * SG_CHUNK, SG_CHUNK), const),
            pl.BlockSpec((SG_CHUNK, SG_WIDTH), const),
        ],
        out_specs=[
            pl.BlockSpec((1, tm, DA_WIDTH), tok),
            pl.BlockSpec((1, tm, DA_WIDTH), tok),
            pl.BlockSpec((1, tm, DA_WIDTH), tok),
            pl.BlockSpec((1, tm, hy_w), tok),
            pl.BlockSpec((1, tm, SG_WIDTH), tok),
        ],
        out_shape=[
            jax.ShapeDtypeStruct((b, s, DA_WIDTH), BF16),
            jax.ShapeDtypeStruct((b, s, DA_WIDTH), BF16),
            jax.ShapeDtypeStruct((b, s, DA_WIDTH), BF16),
            jax.ShapeDtypeStruct((b, s, hy_w), F32),
            jax.ShapeDtypeStruct((b, s, SG_WIDTH), BF16),
        ],
        compiler_params=_cparams("parallel", "parallel"),
        name="mixer_in_rope" if rope else "mixer_in",
    )(x, mod, gain.reshape(1, d), w_in, qk_gain_row, cos_t, sin_t, ones_bd,
      ln_g.reshape(1, -1), ln_b.reshape(1, -1), ws_stack, bs_tile)


def _attn_kernel(bound_ref, q_ref, k_ref, vt_ref, lv_ref, sub_ref, o_ref, p_ref, acc_ref, *, lam_init):
    lv = lv_ref[...]
    lam = (jnp.exp(jnp.sum(lv[0:1] * lv[1:2], axis=-1, keepdims=True))
           - jnp.exp(jnp.sum(lv[2:3] * lv[3:4], axis=-1, keepdims=True)) + lam_init)
    q = q_ref[0]
    tq = q.shape[0]
    nblk = k_ref.shape[1] // ATTN_KEY_BLOCK
    nt = (((1,), (1,)), ((), ()))
    bound = bound_ref[...]

    def block_scores(j, c):
        start = pl.multiple_of(j * ATTN_KEY_BLOCK, ATTN_KEY_BLOCK)
        cols = slice(c * DA_DH, (c + 1) * DA_DH)
        kblk = k_ref[0, pl.ds(start, ATTN_KEY_BLOCK), cols]
        return start, lax.dot_general(kblk, q[:, cols], nt, preferred_element_type=F32)

    def fast_block(j, carry):
        for c in range(2):
            start, s = block_scores(j, c)
            p_ref[c, pl.ds(start, ATTN_KEY_BLOCK), :] = jnp.exp2(s - bound).astype(BF16)
        return carry

    lax.fori_loop(0, nblk, fast_block, 0, unroll=True)
    for c in range(2):
        acc_ref[c] = _dot(vt_ref[0], p_ref[c])
    mass = jnp.minimum(acc_ref[0, DA_DV:DA_DV + 1, :], acc_ref[1, DA_DV:DA_DV + 1, :])

    @pl.when(jnp.min(mass) < ATTN_MIN_MASS)
    def _():
        def slow_block(j, maxima):
            out = []
            for c in range(2):
                start, s = block_scores(j, c)
                m_new = jnp.maximum(maxima[c], jnp.max(s, axis=0, keepdims=True))
                vblk = vt_ref[0, :, pl.ds(start, ATTN_KEY_BLOCK)]
                acc_ref[c] = (jnp.exp2(maxima[c] - m_new) * acc_ref[c]
                              + _dot(vblk, jnp.exp2(s - m_new).astype(BF16)))
                out.append(m_new)
            return tuple(out)

        acc_ref[...] = jnp.zeros(acc_ref.shape, F32)
        start_max = jnp.full((1, tq), ATTN_M_INIT, F32)
        lax.fori_loop(0, nblk, slow_block, (start_max, start_max))

    l0 = acc_ref[0, DA_DV:DA_DV + 1, :]
    l1 = acc_ref[1, DA_DV:DA_DV + 1, :]
    o = acc_ref[0, :DA_DV, :] * (1.0 / l0) - acc_ref[1, :DA_DV, :] * (lam / l1)
    y = o * lax.rsqrt(jnp.mean(o * o, axis=0, keepdims=True) + EPS)
    sub = jnp.concatenate([sub_ref[...]] * (tq // LANES), axis=-1)
    o_ref[0] = (y * sub * (1.0 - lam_init)).T.astype(BF16)


def _attention(q, k, v, qk_gain, lam_vecs, subln, layer_idx):
    b, sq, _ = q.shape
    sk = k.shape[1]
    tq = min(sq, ATTN_Q_TILE)
    lam_init = 0.8 - 0.6 * math.exp(-0.3 * layer_idx)
    bound = (ATTN_BOUND_MARGIN * Q_SCALE * DA_DH) * jnp.max(jnp.abs(qk_gain[0])) * jnp.max(jnp.abs(qk_gain[1]))
    vt = jnp.swapaxes(v.reshape(b, sk, DA_HEADS, DA_DV), 1, 3).swapaxes(1, 2)
    extra = jnp.zeros((b, DA_HEADS, ATTN_VT_ROWS - DA_DV, sk), BF16).at[:, :, 0].set(1.0)
    vt = jnp.concatenate([vt, extra], axis=2).reshape(b, DA_HEADS * ATTN_VT_ROWS, sk)
    return pl.pallas_call(
        functools.partial(_attn_kernel, lam_init=lam_init),
        grid=(b, DA_HEADS, sq // tq),
        in_specs=[
            pl.BlockSpec((1, 1), lambda i, h, j: (0, 0)),
            pl.BlockSpec((1, tq, DA_DV), lambda i, h, j: (i, j, h)),
            pl.BlockSpec((1, sk, DA_DV), lambda i, h, j: (i, 0, h)),
            pl.BlockSpec((1, ATTN_VT_ROWS, sk), lambda i, h, j: (i, h, 0)),
            pl.BlockSpec((4, DA_DH), lambda i, h, j: (0, 0)),
            pl.BlockSpec((DA_DV, LANES), lambda i, h, j: (0, 0)),
        ],
        out_specs=pl.BlockSpec((1, tq, DA_DV), lambda i, h, j: (i, j, h)),
        out_shape=jax.ShapeDtypeStruct((b, sq, DA_WIDTH), BF16),
        scratch_shapes=[pltpu.VMEM((2, sk, tq), BF16),
                        pltpu.VMEM((2, ATTN_VT_ROWS, tq), F32)],
        compiler_params=_cparams("parallel", "parallel", "parallel"),
        name="diff_attention",
    )(bound.reshape(1, 1), q, k, vt, lam_vecs, jnp.broadcast_to(subln[:, None], (DA_DV, LANES)))


def _short_conv_kernel(p0_ref, p1_ref, p2_ref, w0_ref, w1_ref, w2_ref, b0_ref, b1_ref, b2_ref,
                       o0_ref, o1_ref, o2_ref):
    n = p0_ref.shape[1]
    row = lax.broadcasted_iota(jnp.int32, (n, LANES), 0)
    for p_ref, w_ref, b_ref, o_ref in ((p0_ref, w0_ref, b0_ref, o0_ref),
                                       (p1_ref, w1_ref, b1_ref, o1_ref),
                                       (p2_ref, w2_ref, b2_ref, o2_ref)):
        p = p_ref[0]
        prev = jnp.where(row == 0, 0.0, pltpu.roll(p, 1, 0))
        nxt = jnp.where(row == n - 1, 0.0, pltpu.roll(p, n - 1, 0))
        w = w_ref[...]
        o_ref[0] = prev * w[0:1] + p * w[1:2] + nxt * w[2:3] + b_ref[...]


def _short_conv(p, conv_w, conv_b):
    b, n, _ = p.shape
    halves = HY_WIDTH // LANES
    seq = lambda part: pl.BlockSpec((1, n, LANES), lambda i, j: (i, 0, part * halves + j))
    wsp = lambda part: pl.BlockSpec((3, LANES), lambda i, j: (0, part * halves + j))
    bsp = lambda part: pl.BlockSpec((1, LANES), lambda i, j: (0, part * halves + j))
    out = pl.BlockSpec((1, n, LANES), lambda i, j: (i, 0, j))
    shp = jax.ShapeDtypeStruct((b, n, HY_WIDTH), F32)
    cb = conv_b.reshape(1, -1)
    return pl.pallas_call(
        _short_conv_kernel,
        grid=(b, halves),
        in_specs=[seq(0), seq(1), seq(2), wsp(0), wsp(1), wsp(2), bsp(0), bsp(1), bsp(2)],
        out_specs=[out, out, out],
        out_shape=[shp, shp, shp],
        compiler_params=_cparams("parallel", "parallel"),
        name="hyena_short_conv",
    )(p, p, p, conv_w, conv_w, conv_w, cb, cb, cb)


def _filter_kernel(z_ref, dec_ref, w1_ref, b1_ref, w2_ref, b2_ref, fr_ref, w3_ref, f_ref, l1_ref):
    i = pl.program_id(0)
    fr = fr_ref[...]
    h = jnp.sin(fr * (_dot3(z_ref[...], w1_ref[...]) + b1_ref[...]))
    h = jnp.sin(fr * (_dot3(h, w2_ref[...]) + b2_ref[...]))
    h = _dot3(h, w3_ref[...])
    tr = h.shape[0]
    half = HY_ORDER * HY_WIDTH
    dec = dec_ref[...]
    fwd = h[:, :half] * dec
    first = (lax.broadcasted_iota(jnp.int32, (tr, half), 0) + i * tr) == 0
    bwd = jnp.where(first, 0.0, h[:, half:] * dec)
    for o in range(HY_ORDER):
        f_ref[o] = fwd[:, o * HY_WIDTH:(o + 1) * HY_WIDTH]
        f_ref[HY_ORDER + o] = bwd[:, o * HY_WIDTH:(o + 1) * HY_WIDTH]
    part = jnp.sum(jnp.abs(fwd) + jnp.abs(bwd), axis=0, keepdims=True)

    @pl.when(i == 0)
    def _():
        l1_ref[...] = part

    @pl.when(i != 0)
    def _():
        l1_ref[...] += part


def _hyena_filter_taps(n, w1, b1, w2, b2, freq, w3):
    t = jnp.linspace(0.0, 1.0, n, dtype=F32)[:, None]
    w = (2.0 * math.pi / n) * jnp.arange(n, dtype=F32)[:, None]
    bands = jnp.linspace(1e-4, HY_BANDS - 1, HY_BANDS, dtype=F32)
    z = jnp.concatenate([t, jnp.cos(bands * w), -jnp.sin(bands * w)], axis=-1)
    emb = 48
    z = jnp.pad(z, ((0, 0), (0, emb - HY_EMB)))
    w1p = jnp.pad(w1, ((0, emb - HY_EMB), (0, 0)))
    max_decay = math.log(HY_TARGET) / HY_FAST_DECAY
    min_decay = math.log(HY_TARGET) / HY_SLOW_DECAY
    deltas = jnp.linspace(min_decay, max_decay, HY_ORDER * HY_WIDTH, dtype=F32)
    decay = jnp.exp(-t * jnp.abs(deltas)[None, :])
    tr = min(n, 512)
    hid = w2.shape[0]
    half = HY_ORDER * HY_WIDTH
    const = lambda i: (0, 0)
    return pl.pallas_call(
        _filter_kernel,
        grid=(n // tr,),
        in_specs=[
            pl.BlockSpec((tr, emb), lambda i: (i, 0)),
            pl.BlockSpec((tr, half), lambda i: (i, 0)),
            pl.BlockSpec((emb, hid), const),
            pl.BlockSpec((1, hid), const),
            pl.BlockSpec((hid, hid), const),
            pl.BlockSpec((1, hid), const),
            pl.BlockSpec((1, hid), const),
            pl.BlockSpec((hid, 2 * half), const),
        ],
        out_specs=[
            pl.BlockSpec((2 * HY_ORDER, tr, HY_WIDTH), lambda i: (0, i, 0)),
            pl.BlockSpec((1, half), const),
        ],
        out_shape=[
            jax.ShapeDtypeStruct((2 * HY_ORDER, n, HY_WIDTH), F32),
            jax.ShapeDtypeStruct((1, half), F32),
        ],
        compiler_params=_cparams("arbitrary"),
        name="hyena_filter_taps",
    )(z, decay, w1p, b1.reshape(1, -1), w2, b2.reshape(1, -1), freq.reshape(1, -1), w3)


@functools.lru_cache(maxsize=None)
def _dft_tables(n1):
    n2 = FFT_N2
    n = n1 * n2
    half = n1 // 2
    k1 = np.arange(half + 1)[:, None].astype(np.float64)
    t1 = np.arange(half)[None, :].astype(np.float64)
    ang_a = 2.0 * np.pi * k1 * t1 / n1
    eye = np.eye(SUBLANES)
    wa_fwd = np.kron(np.concatenate([np.cos(ang_a), -np.sin(ang_a)], axis=0), eye)
    weight = np.where(k1[:half] == 0, 1.0, 2.0)
    wa_inv = np.kron(np.concatenate([(weight * np.cos(ang_a[:half])).T,
                                     (-weight * np.sin(ang_a[:half])).T], axis=1), eye)
    kk1 = np.arange(half + 1)[:, None, None].astype(np.float64)
    kk2 = np.arange(n2)[None, :, None].astype(np.float64)
    tt2 = np.arange(n2)[None, None, :].astype(np.float64)
    ang_b = 2.0 * np.pi * tt2 * (kk1 + n1 * kk2) / n
    er, ei = np.cos(ang_b), -np.sin(ang_b)
    mf = np.concatenate([np.concatenate([er, -ei], axis=2),
                         np.concatenate([ei, er], axis=2)], axis=1)
    mi = np.swapaxes(mf, 1, 2)
    f32 = lambda a: np.ascontiguousarray(a.astype(np.float32))
    return f32(wa_fwd), f32(wa_inv), f32(mf), f32(mi)


def _tile_rows(ref, g, count):
    start = pl.multiple_of(g * SUBLANES, SUBLANES)
    return jnp.concatenate([ref[pl.ds(t * FFT_N2 + start, SUBLANES), :] for t in range(count)], axis=0)


def _dft_forward(u_ref, a_ref, wa_ref, mf_ref, half, emit):
    ks = half + 1

    def stage_a(g, carry):
        start = pl.multiple_of(g * SUBLANES, SUBLANES)
        r = _dot(wa_ref[...], _tile_rows(u_ref, g, half).astype(BF16))
        for part in range(2):
            for k1 in range(ks):
                src = (part * ks + k1) * SUBLANES
                a_ref[part, pl.ds(k1 * FFT_N2 + start, SUBLANES), :] = r[src:src + SUBLANES]
        return carry

    lax.fori_loop(0, FFT_N2 // SUBLANES, stage_a, 0, unroll=2)

    group = 3 if ks % 3 == 0 else 1

    def stage_b(i, carry):
        base = pl.multiple_of(i * (group * FFT_N2), FFT_N2)
        rows = [pl.ds(base + j * FFT_N2, FFT_N2) for j in range(group)]
        xs = [_dot(mf_ref[i * group + j],
                   jnp.concatenate([a_ref[0, rows[j], :], a_ref[1, rows[j], :]], axis=0).astype(BF16))
              for j in range(group)]
        emit(i * group, rows, xs)
        return carry

    lax.fori_loop(0, ks // group, stage_b, 0)


def _spectrum_kernel(t_ref, wa_ref, mf_ref, l1_ref, h_ref, a_ref, *, half, scale):
    d = pl.program_id(1)
    inv = scale / l1_ref[0]

    def emit(k0, rows, xs):
        @pl.when(d == 0)
        def _():
            for j, x in enumerate(xs):
                h_ref[0, 0, k0 + j] = x[:FFT_N2] * inv
                h_ref[0, 1, k0 + j] = x[FFT_N2:] * inv

        @pl.when(d != 0)
        def _():
            for j, x in enumerate(xs):
                h_ref[0, 0, k0 + j] += x[:FFT_N2] * inv
                h_ref[0, 1, k0 + j] -= x[FFT_N2:] * inv

    _dft_forward(t_ref, a_ref, wa_ref, mf_ref, half, emit)


def _filter_spectrum(taps, wa_fwd, mf, l1, n1):
    _, n, c = taps.shape
    half = n1 // 2
    ks = half + 1
    return pl.pallas_call(
        functools.partial(_spectrum_kernel, half=half, scale=1.0 / (n1 * FFT_N2)),
        grid=(HY_ORDER, 2),
        in_specs=[pl.BlockSpec((None, n, c), lambda o, d: (d * HY_ORDER + o, 0, 0)),
                  pl.BlockSpec(wa_fwd.shape, lambda o, d: (0, 0)),
                  _resident(mf.shape, lambda o, d: (0, 0, 0)),
                  pl.BlockSpec((1, 1, c), lambda o, d: (o, 0, 0))],
        out_specs=pl.BlockSpec((1, 2, ks, FFT_N2, c), lambda o, d: (o, 0, 0, 0, 0)),
        out_shape=jax.ShapeDtypeStruct((HY_ORDER, 2, ks, FFT_N2, c), F32),
        scratch_shapes=[pltpu.VMEM((2, ks * FFT_N2, c), F32)],
        compiler_params=_cparams("parallel", "arbitrary"),
        name="hyena_filter_spectrum",
    )(taps, wa_fwd, mf, l1.reshape(HY_ORDER, 1, c))


def _long_conv_kernel(u_ref, g_ref, skip_ref, wa_ref, wi_ref, mf_ref, mi_ref, h_ref, o_ref, a_ref, *, half):
    u_ref, g_ref, o_ref = u_ref.at[0], g_ref.at[0], o_ref.at[0]

    def emit(k0, rows, xs):
        cs = []
        for j, x in enumerate(xs):
            xr, xi = x[:FFT_N2], x[FFT_N2:]
            hr, hi = h_ref[0, k0 + j], h_ref[1, k0 + j]
            y = jnp.concatenate([xr * hr - xi * hi, xr * hi + xi * hr], axis=0).astype(BF16)
            cs.append(_dot(mi_ref[k0 + j], y))
        for j, c in enumerate(cs):
            a_ref[0, rows[j], :] = c[:FFT_N2]
            a_ref[1, rows[j], :] = c[FFT_N2:]

    _dft_forward(u_ref, a_ref, wa_ref, mf_ref, half, emit)

    tile_rows = half * SUBLANES
    alt = (1 - 2 * ((lax.broadcasted_iota(jnp.int32, (tile_rows, 1), 0) // SUBLANES) & 1)).astype(F32)
    skip = skip_ref[...]

    def stage_a_inv(g, carry):
        start = pl.multiple_of(g * SUBLANES, SUBLANES)
        c = jnp.concatenate([_tile_rows(a_ref.at[0], g, half), _tile_rows(a_ref.at[1], g, half)], axis=0)
        nyq = a_ref[0, pl.ds(half * FFT_N2 + start, SUBLANES), :]
        y = _dot(wi_ref[...], c.astype(BF16)) + alt * jnp.concatenate([nyq] * half, axis=0)
        out = _tile_rows(g_ref, g, half) * (y + _tile_rows(u_ref, g, half) * skip)
        for t1 in range(half):
            o_ref[pl.ds(t1 * FFT_N2 + start, SUBLANES), :] = out[t1 * SUBLANES:(t1 + 1) * SUBLANES]
        return carry

    lax.fori_loop(0, FFT_N2 // SUBLANES, stage_a_inv, 0, unroll=2)


def _long_conv(u, gate, skip_row, tables, hspec, order, n1):
    b, n, c = u.shape
    half = n1 // 2
    ks = half + 1
    wa_fwd, wa_inv, mf, mi = tables
    seq = pl.BlockSpec((1, n, c), lambda i: (i, 0, 0))
    const2 = lambda i: (0, 0)
    const3 = lambda i: (0, 0, 0)
    return pl.pallas_call(
        functools.partial(_long_conv_kernel, half=half),
        grid=(b,),
        in_specs=[seq, seq,
                  pl.BlockSpec((1, c), const2),
                  pl.BlockSpec(wa_fwd.shape, const2),
                  pl.BlockSpec(wa_inv.shape, const2),
                  _resident(mf.shape, const3),
                  _resident(mi.shape, const3),
                  _resident((None, 2, ks, FFT_N2, c), lambda i: (order, 0, 0, 0, 0))],
        out_specs=seq,
        out_shape=jax.ShapeDtypeStruct(u.shape, F32),
        scratch_shapes=[pltpu.VMEM((2, ks * FFT_N2, c), F32)],
        compiler_params=_cparams("parallel", vmem=LONG_CONV_VMEM_BYTES),
        name="hyena_long_conv",
    )(u, gate, skip_row, wa_fwd, wa_inv, mf, mi, hspec)


def _hyena(p, conv_w, conv_b, filt_params, skip, tables):
    b, n, _ = p.shape
    n1 = 2 * n // FFT_N2
    wa_fwd, _, mf, _ = tables
    v, x1, x2 = _short_conv(p, conv_w, conv_b)
    taps, l1 = _hyena_filter_taps(n, *filt_params)
    hspec = _filter_spectrum(taps, wa_fwd, mf, l1, n1)
    z = _long_conv(v, x1, skip[0:1], tables, hspec, 0, n1)
    return _long_conv(z, x2, skip[1:2], tables, hspec, 1, n1)


def _merge_kernel(x_ref, mod_ref, g_ref, ya_ref, yb_ref, yc_ref, gw_ref, gb_ref, wbr_ref, wo_ref, o_ref):
    x = x_ref[0]
    d = x.shape[1]
    h = _adaln(x, g_ref[...], mod_ref[0, 3:4, :], mod_ref[0, 4:5, :]).astype(BF16)
    gates = _sigmoid(_dot(h, gw_ref[...]) + gb_ref[...])
    m = (gates[:, :d] * _dot(ya_ref[0], wbr_ref[:DA_WIDTH])
         + gates[:, d:2 * d] * _dot(yb_ref[0].astype(BF16), wbr_ref[DA_WIDTH:DA_WIDTH + HY_WIDTH])
         + gates[:, 2 * d:] * _dot(yc_ref[0], wbr_ref[DA_WIDTH + HY_WIDTH:]))
    o_ref[0] = x + mod_ref[0, 5:6, :] * _dot(m.astype(BF16), wo_ref[...])


def _merge(x, mod, shared_mod, gain, ya, yb, yc, gate_w, gate_b, w_br, w_o):
    b, s, d = x.shape
    tm = min(s, 512)
    mod_map = (lambda i, j: (0, 0, 0)) if shared_mod else (lambda i, j: (i, 0, 0))
    const = lambda i, j: (0, 0)
    tok = lambda i, j: (i, j, 0)
    return pl.pallas_call(
        _merge_kernel,
        grid=(b, s // tm),
        in_specs=[
            pl.BlockSpec((1, tm, d), tok),
            pl.BlockSpec((1, N_MOD, d), mod_map),
            pl.BlockSpec((1, d), const),
            pl.BlockSpec((1, tm, DA_WIDTH), tok),
            pl.BlockSpec((1, tm, HY_WIDTH), tok),
            pl.BlockSpec((1, tm, SG_WIDTH), tok),
            _resident(gate_w.shape, const),
            pl.BlockSpec((1, gate_w.shape[1]), const),
            _resident(w_br.shape, const),
            _resident(w_o.shape, const),
        ],
        out_specs=pl.BlockSpec((1, tm, d), tok),
        out_shape=jax.ShapeDtypeStruct((b, s, d), F32),
        compiler_params=_cparams("parallel", "parallel"),
        name="branch_merge",
    )(x, mod, gain.reshape(1, d), ya, yb, yc, gate_w, gate_b.reshape(1, -1), w_br, w_o)


def _rope_tables(n_tokens):
    t = np.arange(n_tokens)
    pos = np.stack([t // GRID_W, t % GRID_W], axis=-1).astype(np.float64)
    inv = ROPE_BASE ** (-np.arange(ROPE_PAIRS, dtype=np.float64) / ROPE_PAIRS)
    ang = pos[:, :, None] * inv
    cos, sin = np.cos(ang), np.sin(ang)
    cos64 = np.concatenate([cos[:, 0], cos[:, 0], cos[:, 1], cos[:, 1]], axis=-1)
    sin64 = np.concatenate([-sin[:, 0], sin[:, 0], -sin[:, 1], sin[:, 1]], axis=-1)
    as_f32 = lambda a: np.tile(a, (1, LANES // DA_DH)).astype(np.float32)
    return as_f32(cos64), as_f32(sin64)


def _device_tables(n1):
    return tuple(jnp.asarray(t).astype(BF16) for t in _dft_tables(n1))


def kernel(x, c, ctx, c_ctx, ada_w, ada_b, norm_g, ffn_up, ffn_down, w_in, da_qk_gain, da_lambda, da_subln, hy_conv_w, hy_conv_b, hy_f_w1, hy_f_b1, hy_f_w2, hy_f_b2, hy_f_freq, hy_f_w3, hy_skip, sg_ln_g, sg_ln_b, sg_w, sg_b, gate_w, gate_b, w_br, w_o):
    batch, seq, d = x.shape
    cos_t, sin_t = _rope_tables(seq)
    ones_bd = np.kron(np.eye(256 // DA_DH), np.ones((DA_DH, DA_DH))).astype(BF16)
    tables_l = _device_tables(2 * seq // FFT_N2)
    tables_c = _device_tables(2 * ctx.shape[1] // FFT_N2)

    cc = jnp.concatenate([c, c_ctx[None], jnp.zeros((MOD_ROWS - batch - 1, d), F32)], axis=0)
    mod = _modulation(cc, ada_w, ada_b).reshape(DEPTH, MOD_ROWS, N_MOD, d)

    ffn_up_b = ffn_up.astype(BF16)
    ffn_down_b = ffn_down.astype(BF16)
    w_in_b = w_in.astype(BF16)
    gate_w_b = gate_w.astype(BF16)
    w_br_b = w_br.astype(BF16)
    w_o_b = w_o.astype(BF16)
    sg_w_b = sg_w.astype(BF16)

    xl, xc = x, ctx
    for l in range(DEPTH):
        last = l == DEPTH - 1
        mod_l = mod[l, :batch]
        mod_c = mod[l, batch:batch + 1]
        filt_params = (hy_f_w1[l], hy_f_b1[l], hy_f_w2[l], hy_f_b2[l], hy_f_freq[l], hy_f_w3[l])
        qk_gain_row = jnp.concatenate([jnp.tile(da_qk_gain[l, 0], 2 * DA_HEADS),
                                       jnp.tile(da_qk_gain[l, 1], 2 * DA_HEADS)]).reshape(1, -1)
        ws_stack = sg_w_b[l].reshape(SG_GROUPS * SG_CHUNK, SG_CHUNK)
        bs_tile = jnp.repeat(sg_b[l].T, SG_WIDTH // SG_GROUPS, axis=1)

        xl = _ffn(xl, mod_l, False, norm_g[l, 0], ffn_up_b[l, 0], ffn_down_b[l, 0], 0)
        xc = _ffn(xc, mod_c, True, norm_g[l, 0], ffn_up_b[l, 0], ffn_down_b[l, 0], 0)

        mix = functools.partial(_mixer_in, gain=norm_g[l, 1], w_in=w_in_b[l], qk_gain_row=qk_gain_row,
                                cos_t=cos_t, sin_t=sin_t, ones_bd=ones_bd, ln_g=sg_ln_g[l], ln_b=sg_ln_b[l],
                                ws_stack=ws_stack, bs_tile=bs_tile)
        ql, kl, vl, hy_l, yc_l = mix(xl, mod_l, False, rope=True)
        qc, kc, vc, hy_c, yc_c = mix(xc, mod_c, True, rope=False)

        k_all = jnp.concatenate([kc, kl], axis=1)
        v_all = jnp.concatenate([vc, vl], axis=1)
        ya_l = _attention(ql, k_all, v_all, da_qk_gain[l], da_lambda[l], da_subln[l], l)
        yb_l = _hyena(hy_l, hy_conv_w[l], hy_conv_b[l], filt_params, hy_skip[l], tables_l)
        merge = functools.partial(_merge, gain=norm_g[l, 1], gate_w=gate_w_b[l], gate_b=gate_b[l],
                                  w_br=w_br_b[l], w_o=w_o_b[l])
        xl = merge(xl, mod_l, False, ya=ya_l, yb=yb_l, yc=yc_l)
        xl = _ffn(xl, mod_l, False, norm_g[l, 2], ffn_up_b[l, 1], ffn_down_b[l, 1], 2)

        if not last:
            ya_c = _attention(qc, kc, vc, da_qk_gain[l], da_lambda[l], da_subln[l], l)
            yb_c = _hyena(hy_c, hy_conv_w[l], hy_conv_b[l], filt_params, hy_skip[l], tables_c)
            xc = merge(xc, mod_c, True, ya=ya_c, yb=yb_c, yc=yc_c)
            xc = _ffn(xc, mod_c, True, norm_g[l, 2], ffn_up_b[l, 1], ffn_down_b[l, 1], 2)
    return xl
```

```python
import functools
import math

import numpy as np
import jax
import jax.numpy as jnp
from jax import lax
from jax.experimental import pallas as pl
from jax.experimental.pallas import tpu as pltpu

F32 = jnp.float32
BF16 = jnp.bfloat16

D_MODEL = 1024
DEPTH = 2
GRID_W = 64
EPS = 1e-6
N_MOD = 9
FFN_HIDDEN = 2816
FFN_CHUNKS = ((0, 1536), (1536, FFN_HIDDEN))

DA_HEADS = 4
DA_DH = 64
DA_DV = 2 * DA_DH
DA_WIDTH = DA_HEADS * DA_DV
ROPE_BASE = 10000.0
ROPE_PAIRS = DA_DH // 4

HY_WIDTH = 256
HY_ORDER = 2
HY_EMB = 33
HY_BANDS = (HY_EMB - 1) // 2
HY_FAST_DECAY = 0.3
HY_SLOW_DECAY = 1.5
HY_TARGET = 1e-2

SG_WIDTH = 256
SG_GROUPS = 4
SG_CHUNK = 128

HY_OFF = 3 * DA_WIDTH
SG_OFF = HY_OFF + (HY_ORDER + 1) * HY_WIDTH
IN_WIDTH = SG_OFF + 2 * SG_WIDTH

VMEM_LIMIT_BYTES = 56 * 1024 * 1024
LONG_CONV_VMEM_BYTES = 60 * 1024 * 1024
LANES = 128
SUBLANES = 8
FFT_N2 = 128
MOD_ROWS = 16
Q_SCALE = math.log2(math.e) * DA_DH ** -0.5
ATTN_Q_TILE = 512
ATTN_KEY_BLOCK = 256
ATTN_VT_ROWS = DA_DV + 16
ATTN_BOUND_MARGIN = 1.02
ATTN_MIN_MASS = 2.0 ** -60
ATTN_M_INIT = -1e30


def _cparams(*sem, vmem=VMEM_LIMIT_BYTES):
    return pltpu.CompilerParams(dimension_semantics=sem, vmem_limit_bytes=vmem)


def _resident(shape, index_map):
    return pl.BlockSpec(shape, index_map, pipeline_mode=pl.Buffered(1))


def _dot(a, b):
    return jnp.dot(a, b, preferred_element_type=F32)


def _split(a):
    hi = a.astype(BF16)
    lo = (a - hi.astype(F32)).astype(BF16)
    return hi, lo


def _dot3(a, b):
    ah, al = _split(a)
    bh, bl = _split(b)
    return _dot(ah, bh) + (_dot(al, bh) + _dot(ah, bl))


def _sigmoid(x):
    return 1.0 / (1.0 + jnp.exp(-x))


def _adaln(x, gain, shift, scale):
    y = x * lax.rsqrt(jnp.mean(x * x, axis=-1, keepdims=True) + EPS)
    return (y * gain) * (1.0 + scale) + shift


def _mod_kernel(c_ref, w_ref, b_ref, o_ref):
    c = c_ref[...]
    s = c * _sigmoid(c)
    o_ref[0] = _dot(s.astype(BF16), w_ref[0].astype(BF16)) + b_ref[0]


def _modulation(cc, ada_w, ada_b):
    depth, d, n = ada_w.shape
    tn = 1536
    return pl.pallas_call(
        _mod_kernel,
        grid=(depth, n // tn),
        in_specs=[
            pl.BlockSpec((MOD_ROWS, d), lambda l, j: (0, 0)),
            pl.BlockSpec((1, d, tn), lambda l, j: (l, 0, j)),
            pl.BlockSpec((1, 1, tn), lambda l, j: (l, 0, j)),
        ],
        out_specs=pl.BlockSpec((1, MOD_ROWS, tn), lambda l, j: (l, 0, j)),
        out_shape=jax.ShapeDtypeStruct((depth, MOD_ROWS, n), F32),
        compiler_params=_cparams("parallel", "parallel"),
        name="modulation",
    )(cc, ada_w, ada_b.reshape(depth, 1, n))


def _ffn_apply(x, mod_ref, sub, gain, wu_ref, wd_ref):
    shift = mod_ref[0, 3 * sub:3 * sub + 1, :]
    scale = mod_ref[0, 3 * sub + 1:3 * sub + 2, :]
    gate = mod_ref[0, 3 * sub + 2:3 * sub + 3, :]
    h = _adaln(x, gain, shift, scale).astype(BF16)
    acc = None
    for lo, hi in FFN_CHUNKS:
        a = _dot(h, wu_ref[:, lo:hi])
        b = _dot(h, wu_ref[:, FFN_HIDDEN + lo:FFN_HIDDEN + hi])
        part = _dot((a * _sigmoid(a) * b).astype(BF16), wd_ref[lo:hi, :])
        acc = part if acc is None else acc + part
    return x + (0.5 * gate) * acc


def _group_sums(sq, ones_bd):
    outs = []
    for j in range(sq.shape[1] // 256):
        hi, lo = _split(sq[:, 256 * j:256 * (j + 1)])
        outs.append(_dot(hi, ones_bd) + _dot(lo, ones_bd))
    return jnp.concatenate(outs, axis=-1)


def _swap16(x):
    n = x.shape[-1]
    lane = lax.broadcasted_iota(jnp.int32, x.shape, x.ndim - 1)
    up = pltpu.roll(x, 16, x.ndim - 1)
    dn = pltpu.roll(x, n - 16, x.ndim - 1)
    return jnp.where((lane & 16) != 0, up, dn)


def _ffn_mixer_kernel(x_ref, mod_ref, g_ref, wu_ref, wd_ref, w_ref, qkg_ref, cos_ref, sin_ref, ones_ref,
                      lng_ref, lnb_ref, ws_ref, bs_ref,
                      x1_ref, q_ref, k_ref, v_ref, hy_ref, yc_ref, *, rope):
    x = _ffn_apply(x_ref[0], mod_ref, 0, g_ref[0:1, :], wu_ref, wd_ref)
    x1_ref[0] = x
    tm = x.shape[0]
    h = _adaln(x, g_ref[1:2, :], mod_ref[0, 3:4, :], mod_ref[0, 4:5, :]).astype(BF16)

    qk = _dot(h, w_ref[:, :2 * DA_WIDTH])
    ss = _group_sums(qk * qk, ones_ref[...])
    qk = qk * lax.rsqrt(ss * (1.0 / DA_DH) + EPS) * qkg_ref[...]
    if rope:
        reps = 2 * DA_WIDTH // LANES
        cosf = jnp.concatenate([cos_ref[...]] * reps, axis=-1)
        sinf = jnp.concatenate([sin_ref[...]] * reps, axis=-1)
        qk = qk * cosf + _swap16(qk) * sinf
    q_ref[0] = (qk[:, :DA_WIDTH] * Q_SCALE).astype(BF16)
    k_ref[0] = qk[:, DA_WIDTH:].astype(BF16)

    v_ref[0] = _dot(h, w_ref[:, 2 * DA_WIDTH:HY_OFF]).astype(BF16)
    hy_ref[0] = _dot(h, w_ref[:, HY_OFF:SG_OFF])

    sg = _dot(h, w_ref[:, SG_OFF:])
    sg = 0.5 * sg * (1.0 + lax.erf(sg * (2.0 ** -0.5)))
    u = sg[:, :SG_WIDTH]
    vv = sg[:, SG_WIDTH:]
    mu = jnp.mean(vv, axis=-1, keepdims=True)
    var = jnp.mean(jnp.square(vv - mu), axis=-1, keepdims=True)
    vv = ((vv - mu) * lax.rsqrt(var + EPS) * lng_ref[...] + lnb_ref[...]).astype(BF16)
    group = lax.broadcasted_iota(jnp.int32, (SG_CHUNK, SG_WIDTH), 1) // (SG_WIDTH // SG_GROUPS)
    for c in range(tm // SG_CHUNK):
        rows = slice(c * SG_CHUNK, (c + 1) * SG_CHUNK)
        mixed = _dot(ws_ref[...], vv[rows])
        sel = mixed[:SG_CHUNK]
        for g in range(1, SG_GROUPS):
            sel = jnp.where(group == g, mixed[g * SG_CHUNK:(g + 1) * SG_CHUNK], sel)
        yc_ref[0, rows, :] = (u[rows] * (sel + bs_ref[...])).astype(BF16)


def _layer_spec(arr, layer, *lead):
    idx = (layer,) + lead
    return _resident((None,) * len(idx) + arr.shape[len(idx):], lambda i, j: idx + (0, 0))


def _mod_spec(layer, row):
    d = D_MODEL
    if row is None:
        return pl.BlockSpec((None, 1, N_MOD, d), lambda i, j: (layer, i, 0, 0))
    return pl.BlockSpec((None, 1, N_MOD, d), lambda i, j: (layer, row, 0, 0))


def _ffn_mixer(x, mod, mod_row, layer, norm_g, ffn_up, ffn_down, w_in, qk_gain_row, cos_t, sin_t, ones_bd,
               ln_g, ln_b, ws_stack, bs_tile, rope):
    b, s, d = x.shape
    tm = min(s, 512)
    const = lambda i, j: (0, 0)
    tok = lambda i, j: (i, j, 0)
    hy_w = SG_OFF - HY_OFF
    return pl.pallas_call(
        functools.partial(_ffn_mixer_kernel, rope=rope),
        grid=(b, s // tm),
        in_specs=[
            pl.BlockSpec((1, tm, d), tok),
            _mod_spec(layer, mod_row),
            pl.BlockSpec((None, 3, d), lambda i, j: (layer, 0, 0)),
            _layer_spec(ffn_up, layer, 0),
            _layer_spec(ffn_down, layer, 0),
            _layer_spec(w_in, layer),
            pl.BlockSpec((1, 2 * DA_WIDTH), const),
            pl.BlockSpec((tm, LANES), lambda i, j: (j, 0)),
            pl.BlockSpec((tm, LANES), lambda i, j: (j, 0)),
            pl.BlockSpec((256, 256), const),
            pl.BlockSpec((1, SG_WIDTH), const),
            pl.BlockSpec((1, SG_WIDTH), const),
            pl.BlockSpec((SG_GROUPS * SG_CHUNK, SG_CHUNK), const),
            pl.BlockSpec((SG_CHUNK, SG_WIDTH), const),
        ],
        out_specs=[
            pl.BlockSpec((1, tm, d), tok),
            pl.BlockSpec((1, tm, DA_WIDTH), tok),
            pl.BlockSpec((1, tm, DA_WIDTH), tok),
            pl.BlockSpec((1, tm, DA_WIDTH), tok),
            pl.BlockSpec((1, tm, hy_w), tok),
            pl.BlockSpec((1, tm, SG_WIDTH), tok),
        ],
        out_shape=[
            jax.ShapeDtypeStruct((b, s, d), F32),
            jax.ShapeDtypeStruct((b, s, DA_WIDTH), BF16),
            jax.ShapeDtypeStruct((b, s, DA_WIDTH), BF16),
            jax.ShapeDtypeStruct((b, s, DA_WIDTH), BF16),
            jax.ShapeDtypeStruct((b, s, hy_w), F32),
            jax.ShapeDtypeStruct((b, s, SG_WIDTH), BF16),
        ],
        compiler_params=_cparams("parallel", "parallel"),
        name="ffn_mixer_in_rope" if rope else "ffn_mixer_in",
    )(x, mod, norm_g, ffn_up, ffn_down, w_in, qk_gain_row, cos_t, sin_t, ones_bd,
      ln_g.reshape(1, -1), ln_b.reshape(1, -1), ws_stack, bs_tile)


def _attn_kernel(bound_ref, q_ref, k_ref, vt_ref, lv_ref, sub_ref, o_ref, p_ref, acc_ref, *, lam_init):
    lv = lv_ref[...]
    lam = (jnp.exp(jnp.sum(lv[0:1] * lv[1:2], axis=-1, keepdims=True))
           - jnp.exp(jnp.sum(lv[2:3] * lv[3:4], axis=-1, keepdims=True)) + lam_init)
    q = q_ref[0]
    tq = q.shape[0]
    nblk = k_ref.shape[1] // ATTN_KEY_BLOCK
    nt = (((1,), (1,)), ((), ()))
    bound = bound_ref[...]

    def block_scores(j, c):
        start = pl.multiple_of(j * ATTN_KEY_BLOCK, ATTN_KEY_BLOCK)
        cols = slice(c * DA_DH, (c + 1) * DA_DH)
        kblk = k_ref[0, pl.ds(start, ATTN_KEY_BLOCK), cols]
        return start, lax.dot_general(kblk, q[:, cols], nt, preferred_element_type=F32)

    def fast_block(j, carry):
        for c in range(2):
            start, s = block_scores(j, c)
            p_ref[c, pl.ds(start, ATTN_KEY_BLOCK), :] = jnp.exp2(s - bound).astype(BF16)
        return carry

    lax.fori_loop(0, nblk, fast_block, 0, unroll=True)
    for c in range(2):
        acc_ref[c] = _dot(vt_ref[0], p_ref[c])
    mass = jnp.minimum(acc_ref[0, DA_DV:DA_DV + 1, :], acc_ref[1, DA_DV:DA_DV + 1, :])

    @pl.when(jnp.min(mass) < ATTN_MIN_MASS)
    def _():
        def slow_block(j, maxima):
            out = []
            for c in range(2):
                start, s = block_scores(j, c)
                m_new = jnp.maximum(maxima[c], jnp.max(s, axis=0, keepdims=True))
                vblk = vt_ref[0, :, pl.ds(start, ATTN_KEY_BLOCK)]
                acc_ref[c] = (jnp.exp2(maxima[c] - m_new) * acc_ref[c]
                              + _dot(vblk, jnp.exp2(s - m_new).astype(BF16)))
                out.append(m_new)
            return tuple(out)

        acc_ref[...] = jnp.zeros(acc_ref.shape, F32)
        start_max = jnp.full((1, tq), ATTN_M_INIT, F32)
        lax.fori_loop(0, nblk, slow_block, (start_max, start_max))

    l0 = acc_ref[0, DA_DV:DA_DV + 1, :]
    l1 = acc_ref[1, DA_DV:DA_DV + 1, :]
    o = acc_ref[0, :DA_DV, :] * (1.0 / l0) - acc_ref[1, :DA_DV, :] * (lam / l1)
    y = o * lax.rsqrt(jnp.mean(o * o, axis=0, keepdims=True) + EPS)
    sub = jnp.concatenate([sub_ref[...]] * (tq // LANES), axis=-1)
    o_ref[0] = (y * sub * (1.0 - lam_init)).T.astype(BF16)


def _attention(q, k_parts, v_parts, qk_gain, lam_vecs, subln, layer_idx):
    b, sq, _ = q.shape
    k = jnp.concatenate(k_parts, axis=1)
    sk = k.shape[1]
    tq = min(sq, ATTN_Q_TILE)
    lam_init = 0.8 - 0.6 * math.exp(-0.3 * layer_idx)
    bound = (ATTN_BOUND_MARGIN * Q_SCALE * DA_DH) * jnp.max(jnp.abs(qk_gain[0])) * jnp.max(jnp.abs(qk_gain[1]))
    vt = jnp.concatenate([jnp.swapaxes(v.reshape(b, v.shape[1], DA_HEADS, DA_DV), 1, 3).swapaxes(1, 2)
                          for v in v_parts], axis=3)
    extra = jnp.zeros((b, DA_HEADS, ATTN_VT_ROWS - DA_DV, sk), BF16).at[:, :, 0].set(1.0)
    vt = jnp.concatenate([vt, extra], axis=2).reshape(b, DA_HEADS * ATTN_VT_ROWS, sk)
    return pl.pallas_call(
        functools.partial(_attn_kernel, lam_init=lam_init),
        grid=(b, DA_HEADS, sq // tq),
        in_specs=[
            pl.BlockSpec((1, 1), lambda i, h, j: (0, 0)),
            pl.BlockSpec((1, tq, DA_DV), lambda i, h, j: (i, j, h)),
            pl.BlockSpec((1, sk, DA_DV), lambda i, h, j: (i, 0, h)),
            pl.BlockSpec((1, ATTN_VT_ROWS, sk), lambda i, h, j: (i, h, 0)),
            pl.BlockSpec((4, DA_DH), lambda i, h, j: (0, 0)),
            pl.BlockSpec((DA_DV, LANES), lambda i, h, j: (0, 0)),
        ],
        out_specs=pl.BlockSpec((1, tq, DA_DV), lambda i, h, j: (i, j, h)),
        out_shape=jax.ShapeDtypeStruct((b, sq, DA_WIDTH), BF16),
        scratch_shapes=[pltpu.VMEM((2, sk, tq), BF16),
                        pltpu.VMEM((2, ATTN_VT_ROWS, tq), F32)],
        compiler_params=_cparams("parallel", "parallel", "parallel"),
        name="diff_attention",
    )(bound.reshape(1, 1), q, k, vt, lam_vecs, jnp.broadcast_to(subln[:, None], (DA_DV, LANES)))


def _short_conv_kernel(p0_ref, p1_ref, p2_ref, w0_ref, w1_ref, w2_ref, b0_ref, b1_ref, b2_ref,
                       o0_ref, o1_ref, o2_ref):
    n = p0_ref.shape[1]
    row = lax.broadcasted_iota(jnp.int32, (n, LANES), 0)
    for p_ref, w_ref, b_ref, o_ref in ((p0_ref, w0_ref, b0_ref, o0_ref),
                                       (p1_ref, w1_ref, b1_ref, o1_ref),
                                       (p2_ref, w2_ref, b2_ref, o2_ref)):
        p = p_ref[0]
        prev = jnp.where(row == 0, 0.0, pltpu.roll(p, 1, 0))
        nxt = jnp.where(row == n - 1, 0.0, pltpu.roll(p, n - 1, 0))
        w = w_ref[...]
        o_ref[0] = prev * w[0:1] + p * w[1:2] + nxt * w[2:3] + b_ref[...]


def _short_conv(p, conv_w, conv_b):
    b, n, _ = p.shape
    halves = HY_WIDTH // LANES
    seq = lambda part: pl.BlockSpec((1, n, LANES), lambda i, j: (i, 0, part * halves + j))
    wsp = lambda part: pl.BlockSpec((3, LANES), lambda i, j: (0, part * halves + j))
    bsp = lambda part: pl.BlockSpec((1, LANES), lambda i, j: (0, part * halves + j))
    out = pl.BlockSpec((1, n, LANES), lambda i, j: (i, 0, j))
    shp = jax.ShapeDtypeStruct((b, n, HY_WIDTH), F32)
    cb = conv_b.reshape(1, -1)
    return pl.pallas_call(
        _short_conv_kernel,
        grid=(b, halves),
        in_specs=[seq(0), seq(1), seq(2), wsp(0), wsp(1), wsp(2), bsp(0), bsp(1), bsp(2)],
        out_specs=[out, out, out],
        out_shape=[shp, shp, shp],
        compiler_params=_cparams("parallel", "parallel"),
        name="hyena_short_conv",
    )(p, p, p, conv_w, conv_w, conv_w, cb, cb, cb)


def _filter_kernel(z_ref, dec_ref, w1_ref, b1_ref, w2_ref, b2_ref, fr_ref, w3_ref, f_ref, l1_ref):
    i = pl.program_id(0)
    fr = fr_ref[...]
    h = jnp.sin(fr * (_dot3(z_ref[...], w1_ref[...]) + b1_ref[...]))
    h = jnp.sin(fr * (_dot3(h, w2_ref[...]) + b2_ref[...]))
    h = _dot3(h, w3_ref[...])
    tr = h.shape[0]
    half = HY_ORDER * HY_WIDTH
    dec = dec_ref[...]
    fwd = h[:, :half] * dec
    first = (lax.broadcasted_iota(jnp.int32, (tr, half), 0) + i * tr) == 0
    bwd = jnp.where(first, 0.0, h[:, half:] * dec)
    for o in range(HY_ORDER):
        f_ref[o] = fwd[:, o * HY_WIDTH:(o + 1) * HY_WIDTH]
        f_ref[HY_ORDER + o] = bwd[:, o * HY_WIDTH:(o + 1) * HY_WIDTH]
    part = jnp.sum(jnp.abs(fwd) + jnp.abs(bwd), axis=0, keepdims=True)

    @pl.when(i == 0)
    def _():
        l1_ref[...] = part

    @pl.when(i != 0)
    def _():
        l1_ref[...] += part


def _hyena_filter_taps(n, w1, b1, w2, b2, freq, w3):
    t = jnp.linspace(0.0, 1.0, n, dtype=F32)[:, None]
    w = (2.0 * math.pi / n) * jnp.arange(n, dtype=F32)[:, None]
    bands = jnp.linspace(1e-4, HY_BANDS - 1, HY_BANDS, dtype=F32)
    z = jnp.concatenate([t, jnp.cos(bands * w), -jnp.sin(bands * w)], axis=-1)
    emb = 48
    z = jnp.pad(z, ((0, 0), (0, emb - HY_EMB)))
    w1p = jnp.pad(w1, ((0, emb - HY_EMB), (0, 0)))
    max_decay = math.log(HY_TARGET) / HY_FAST_DECAY
    min_decay = math.log(HY_TARGET) / HY_SLOW_DECAY
    deltas = jnp.linspace(min_decay, max_decay, HY_ORDER * HY_WIDTH, dtype=F32)
    decay = jnp.exp(-t * jnp.abs(deltas)[None, :])
    tr = min(n, 512)
    hid = w2.shape[0]
    half = HY_ORDER * HY_WIDTH
    const = lambda i: (0, 0)
    return pl.pallas_call(
        _filter_kernel,
        grid=(n // tr,),
        in_specs=[
            pl.BlockSpec((tr, emb), lambda i: (i, 0)),
            pl.BlockSpec((tr, half), lambda i: (i, 0)),
            pl.BlockSpec((emb, hid), const),
            pl.BlockSpec((1, hid), const),
            pl.BlockSpec((hid, hid), const),
            pl.BlockSpec((1, hid), const),
            pl.BlockSpec((1, hid), const),
            pl.BlockSpec((hid, 2 * half), const),
        ],
        out_specs=[
            pl.BlockSpec((2 * HY_ORDER, tr, HY_WIDTH), lambda i: (0, i, 0)),
            pl.BlockSpec((1, half), const),
        ],
        out_shape=[
            jax.ShapeDtypeStruct((2 * HY_ORDER, n, HY_WIDTH), F32),
            jax.ShapeDtypeStruct((1, half), F32),
        ],
        compiler_params=_cparams("arbitrary"),
        name="hyena_filter_taps",
    )(z, decay, w1p, b1.reshape(1, -1), w2, b2.reshape(1, -1), freq.reshape(1, -1), w3)


@functools.lru_cache(maxsize=None)
def _dft_tables(n1):
    n2 = FFT_N2
    n = n1 * n2
    half = n1 // 2
    k1 = np.arange(half + 1)[:, None].astype(np.float64)
    t1 = np.arange(half)[None, :].astype(np.float64)
    ang_a = 2.0 * np.pi * k1 * t1 / n1
    eye = np.eye(SUBLANES)
    wa_fwd = np.kron(np.concatenate([np.cos(ang_a), -np.sin(ang_a)], axis=0), eye)
    weight = np.where(k1[:half] == 0, 1.0, 2.0)
    wa_inv = np.kron(np.concatenate([(weight * np.cos(ang_a[:half])).T,
                                     (-weight * np.sin(ang_a[:half])).T], axis=1), eye)
    kk1 = np.arange(half + 1)[:, None, None].astype(np.float64)
    kk2 = np.arange(n2)[None, :, None].astype(np.float64)
    tt2 = np.arange(n2)[None, None, :].astype(np.float64)
    ang_b = 2.0 * np.pi * tt2 * (kk1 + n1 * kk2) / n
    er, ei = np.cos(ang_b), -np.sin(ang_b)
    mf = np.concatenate([np.concatenate([er, -ei], axis=2),
                         np.concatenate([ei, er], axis=2)], axis=1)
    mi = np.swapaxes(mf, 1, 2)
    f32 = lambda a: np.ascontiguousarray(a.astype(np.float32))
    return f32(wa_fwd), f32(wa_inv), f32(mf), f32(mi)


def _tile_rows(ref, g, count):
    start = pl.multiple_of(g * SUBLANES, SUBLANES)
    return jnp.concatenate([ref[pl.ds(t * FFT_N2 + start, SUBLANES), :] for t in range(count)], axis=0)


def _dft_forward(u_ref, a_ref, wa_ref, mf_ref, half, emit):
    ks = half + 1

    def stage_a(g, carry):
        start = pl.multiple_of(g * SUBLANES, SUBLANES)
        r = _dot(wa_ref[...], _tile_rows(u_ref, g, half).astype(BF16))
        for part in range(2):
            for k1 in range(ks):
                src = (part * ks + k1) * SUBLANES
                a_ref[part, pl.ds(k1 * FFT_N2 + start, SUBLANES), :] = r[src:src + SUBLANES]
        return carry

    lax.fori_loop(0, FFT_N2 // SUBLANES, stage_a, 0, unroll=2)

    group = 3 if ks % 3 == 0 else 1

    def stage_b(i, carry):
        base = pl.multiple_of(i * (group * FFT_N2), FFT_N2)
        rows = [pl.ds(base + j * FFT_N2, FFT_N2) for j in range(group)]
        xs = [_dot(mf_ref[i * group + j],
                   jnp.concatenate([a_ref[0, rows[j], :], a_ref[1, rows[j], :]], axis=0).astype(BF16))
              for j in range(group)]
        emit(i * group, rows, xs)
        return carry

    lax.fori_loop(0, ks // group, stage_b, 0)


def _spectrum_kernel(t_ref, wa_ref, mf_ref, l1_ref, h_ref, a_ref, *, half, scale):
    d = pl.program_id(1)
    inv = scale / l1_ref[0]

    def emit(k0, rows, xs):
        @pl.when(d == 0)
        def _():
            for j, x in enumerate(xs):
                h_ref[0, 0, k0 + j] = x[:FFT_N2] * inv
                h_ref[0, 1, k0 + j] = x[FFT_N2:] * inv

        @pl.when(d != 0)
        def _():
            for j, x in enumerate(xs):
                h_ref[0, 0, k0 + j] += x[:FFT_N2] * inv
                h_ref[0, 1, k0 + j] -= x[FFT_N2:] * inv

    _dft_forward(t_ref, a_ref, wa_ref, mf_ref, half, emit)


def _filter_spectrum(taps, wa_fwd, mf, l1, n1):
    _, n, c = taps.shape
    half = n1 // 2
    ks = half + 1
    return pl.pallas_call(
        functools.partial(_spectrum_kernel, half=half, scale=1.0 / (n1 * FFT_N2)),
        grid=(HY_ORDER, 2),
        in_specs=[pl.BlockSpec((None, n, c), lambda o, d: (d * HY_ORDER + o, 0, 0)),
                  pl.BlockSpec(wa_fwd.shape, lambda o, d: (0, 0)),
                  _resident(mf.shape, lambda o, d: (0, 0, 0)),
                  pl.BlockSpec((1, 1, c), lambda o, d: (o, 0, 0))],
        out_specs=pl.BlockSpec((1, 2, ks, FFT_N2, c), lambda o, d: (o, 0, 0, 0, 0)),
        out_shape=jax.ShapeDtypeStruct((HY_ORDER, 2, ks, FFT_N2, c), F32),
        scratch_shapes=[pltpu.VMEM((2, ks * FFT_N2, c), F32)],
        compiler_params=_cparams("parallel", "arbitrary"),
        name="hyena_filter_spectrum",
    )(taps, wa_fwd, mf, l1.reshape(HY_ORDER, 1, c))


def _long_conv_kernel(u_ref, g_ref, skip_ref, wa_ref, wi_ref, mf_ref, mi_ref, h_ref, o_ref, a_ref, *, half):
    u_ref, g_ref, o_ref = u_ref.at[0], g_ref.at[0], o_ref.at[0]

    def emit(k0, rows, xs):
        cs = []
        for j, x in enumerate(xs):
            xr, xi = x[:FFT_N2], x[FFT_N2:]
            hr, hi = h_ref[0, k0 + j], h_ref[1, k0 + j]
            y = jnp.concatenate([xr * hr - xi * hi, xr * hi + xi * hr], axis=0).astype(BF16)
            cs.append(_dot(mi_ref[k0 + j], y))
        for j, c in enumerate(cs):
            a_ref[0, rows[j], :] = c[:FFT_N2]
            a_ref[1, rows[j], :] = c[FFT_N2:]

    _dft_forward(u_ref, a_ref, wa_ref, mf_ref, half, emit)

    tile_rows = half * SUBLANES
    alt = (1 - 2 * ((lax.broadcasted_iota(jnp.int32, (tile_rows, 1), 0) // SUBLANES) & 1)).astype(F32)
    skip = skip_ref[...]

    def stage_a_inv(g, carry):
        start = pl.multiple_of(g * SUBLANES, SUBLANES)
        c = jnp.concatenate([_tile_rows(a_ref.at[0], g, half), _tile_rows(a_ref.at[1], g, half)], axis=0)
        nyq = a_ref[0, pl.ds(half * FFT_N2 + start, SUBLANES), :]
        y = _dot(wi_ref[...], c.astype(BF16)) + alt * jnp.concatenate([nyq] * half, axis=0)
        out = _tile_rows(g_ref, g, half) * (y + _tile_rows(u_ref, g, half) * skip)
        for t1 in range(half):
            o_ref[pl.ds(t1 * FFT_N2 + start, SUBLANES), :] = out[t1 * SUBLANES:(t1 + 1) * SUBLANES]
        return carry

    lax.fori_loop(0, FFT_N2 // SUBLANES, stage_a_inv, 0, unroll=2)


def _long_conv(u, gate, skip_row, tables, hspec, order, n1):
    b, n, c = u.shape
    half = n1 // 2
    ks = half + 1
    wa_fwd, wa_inv, mf, mi = tables
    seq = pl.BlockSpec((1, n, c), lambda i: (i, 0, 0))
    const2 = lambda i: (0, 0)
    const3 = lambda i: (0, 0, 0)
    return pl.pallas_call(
        functools.partial(_long_conv_kernel, half=half),
        grid=(b,),
        in_specs=[seq, seq,
                  pl.BlockSpec((1, c), const2),
                  pl.BlockSpec(wa_fwd.shape, const2),
                  pl.BlockSpec(wa_inv.shape, const2),
                  _resident(mf.shape, const3),
                  _resident(mi.shape, const3),
                  _resident((None, 2, ks, FFT_N2, c), lambda i: (order, 0, 0, 0, 0))],
        out_specs=seq,
        out_shape=jax.ShapeDtypeStruct(u.shape, F32),
        scratch_shapes=[pltpu.VMEM((2, ks * FFT_N2, c), F32)],
        compiler_params=_cparams("parallel", vmem=LONG_CONV_VMEM_BYTES),
        name="hyena_long_conv",
    )(u, gate, skip_row, wa_fwd, wa_inv, mf, mi, hspec)


def _hyena(p, conv_w, conv_b, filt_params, skip, tables):
    b, n, _ = p.shape
    n1 = 2 * n // FFT_N2
    wa_fwd, _, mf, _ = tables
    v, x1, x2 = _short_conv(p, conv_w, conv_b)
    taps, l1 = _hyena_filter_taps(n, *filt_params)
    hspec = _filter_spectrum(taps, wa_fwd, mf, l1, n1)
    z = _long_conv(v, x1, skip[0:1], tables, hspec, 0, n1)
    return _long_conv(z, x2, skip[1:2], tables, hspec, 1, n1)


def _merge_ffn_kernel(x_ref, mod_ref, g_ref, ya_ref, yb_ref, yc_ref, gw_ref, gb_ref, wbr_ref, wo_ref,
                      wu_ref, wd_ref, o_ref):
    x = x_ref[0]
    d = x.shape[1]
    h = _adaln(x, g_ref[1:2, :], mod_ref[0, 3:4, :], mod_ref[0, 4:5, :]).astype(BF16)
    branches = ((ya_ref[0], 0, DA_WIDTH),
                (yb_ref[0].astype(BF16), DA_WIDTH, DA_WIDTH + HY_WIDTH),
                (yc_ref[0], DA_WIDTH + HY_WIDTH, DA_WIDTH + HY_WIDTH + SG_WIDTH))
    m = None
    for i, (y, lo, hi) in enumerate(branches):
        gate = _sigmoid(_dot(h, gw_ref[:, i * d:(i + 1) * d]) + gb_ref[:, i * d:(i + 1) * d])
        part = gate * _dot(y, wbr_ref[lo:hi, :])
        m = part if m is None else m + part
    x = x + mod_ref[0, 5:6, :] * _dot(m.astype(BF16), wo_ref[...])
    o_ref[0] = _ffn_apply(x, mod_ref, 2, g_ref[2:3, :], wu_ref, wd_ref)


def _merge_ffn(x, mod, mod_row, layer, norm_g, ya, yb, yc, gate_w, gate_b, w_br, w_o, ffn_up, ffn_down):
    b, s, d = x.shape
    tm = min(s, 512)
    tok = lambda i, j: (i, j, 0)
    return pl.pallas_call(
        _merge_ffn_kernel,
        grid=(b, s // tm),
        in_specs=[
            pl.BlockSpec((1, tm, d), tok),
            _mod_spec(layer, mod_row),
            pl.BlockSpec((None, 3, d), lambda i, j: (layer, 0, 0)),
            pl.BlockSpec((1, tm, DA_WIDTH), tok),
            pl.BlockSpec((1, tm, HY_WIDTH), tok),
            pl.BlockSpec((1, tm, SG_WIDTH), tok),
            _layer_spec(gate_w, layer),
            pl.BlockSpec((None, 1, gate_w.shape[2]), lambda i, j: (layer, 0, 0)),
            _layer_spec(w_br, layer),
            _layer_spec(w_o, layer),
            _layer_spec(ffn_up, layer, 1),
            _layer_spec(ffn_down, layer, 1),
        ],
        out_specs=pl.BlockSpec((1, tm, d), tok),
        out_shape=jax.ShapeDtypeStruct((b, s, d), F32),
        compiler_params=_cparams("parallel", "parallel"),
        name="merge_ffn",
    )(x, mod, norm_g, ya, yb, yc, gate_w, gate_b.reshape(gate_b.shape[0], 1, -1), w_br, w_o, ffn_up, ffn_down)


def _rope_tables(n_tokens):
    t = np.arange(n_tokens)
    pos = np.stack([t // GRID_W, t % GRID_W], axis=-1).astype(np.float64)
    inv = ROPE_BASE ** (-np.arange(ROPE_PAIRS, dtype=np.float64) / ROPE_PAIRS)
    ang = pos[:, :, None] * inv
    cos, sin = np.cos(ang), np.sin(ang)
    cos64 = np.concatenate([cos[:, 0], cos[:, 0], cos[:, 1], cos[:, 1]], axis=-1)
    sin64 = np.concatenate([-sin[:, 0], sin[:, 0], -sin[:, 1], sin[:, 1]], axis=-1)
    as_f32 = lambda a: np.tile(a, (1, LANES // DA_DH)).astype(np.float32)
    return as_f32(cos64), as_f32(sin64)


def _device_tables(n1):
    return tuple(jnp.asarray(t).astype(BF16) for t in _dft_tables(n1))


def kernel(x, c, ctx, c_ctx, ada_w, ada_b, norm_g, ffn_up, ffn_down, w_in, da_qk_gain, da_lambda, da_subln, hy_conv_w, hy_conv_b, hy_f_w1, hy_f_b1, hy_f_w2, hy_f_b2, hy_f_freq, hy_f_w3, hy_skip, sg_ln_g, sg_ln_b, sg_w, sg_b, gate_w, gate_b, w_br, w_o):
    batch, seq, d = x.shape
    cos_t, sin_t = _rope_tables(seq)
    ones_bd = np.kron(np.eye(256 // DA_DH), np.ones((DA_DH, DA_DH))).astype(BF16)
    tables_l = _device_tables(2 * seq // FFT_N2)
    tables_c = _device_tables(2 * ctx.shape[1] // FFT_N2)

    cc = jnp.concatenate([c, c_ctx[None], jnp.zeros((MOD_ROWS - batch - 1, d), F32)], axis=0)
    mod = _modulation(cc, ada_w, ada_b).reshape(DEPTH, MOD_ROWS, N_MOD, d)

    ffn_up_b = ffn_up.astype(BF16)
    ffn_down_b = ffn_down.astype(BF16)
    w_in_b = w_in.astype(BF16)
    gate_w_b = gate_w.astype(BF16)
    w_br_b = w_br.astype(BF16)
    w_o_b = w_o.astype(BF16)
    sg_w_b = sg_w.astype(BF16)

    xl, xc = x, ctx
    ctx_row = batch
    for l in range(DEPTH):
        last = l == DEPTH - 1
        filt_params = (hy_f_w1[l], hy_f_b1[l], hy_f_w2[l], hy_f_b2[l], hy_f_freq[l], hy_f_w3[l])
        qk_gain_row = jnp.concatenate([jnp.tile(da_qk_gain[l, 0], 2 * DA_HEADS),
                                       jnp.tile(da_qk_gain[l, 1], 2 * DA_HEADS)]).reshape(1, -1)
        ws_stack = sg_w_b[l].reshape(SG_GROUPS * SG_CHUNK, SG_CHUNK)
        bs_tile = jnp.repeat(sg_b[l].T, SG_WIDTH // SG_GROUPS, axis=1)

        head = functools.partial(_ffn_mixer, mod=mod, layer=l, norm_g=norm_g, ffn_up=ffn_up_b, ffn_down=ffn_down_b,
                                 w_in=w_in_b, qk_gain_row=qk_gain_row, cos_t=cos_t, sin_t=sin_t, ones_bd=ones_bd,
                                 ln_g=sg_ln_g[l], ln_b=sg_ln_b[l], ws_stack=ws_stack, bs_tile=bs_tile)
        xl, ql, kl, vl, hy_l, yc_l = head(xl, mod_row=None, rope=True)
        xc, qc, kc, vc, hy_c, yc_c = head(xc, mod_row=ctx_row, rope=False)

        ya_l = _attention(ql, (kl, kc), (vl, vc), da_qk_gain[l], da_lambda[l], da_subln[l], l)
        yb_l = _hyena(hy_l, hy_conv_w[l], hy_conv_b[l], filt_params, hy_skip[l], tables_l)
        tail = functools.partial(_merge_ffn, mod=mod, layer=l, norm_g=norm_g, gate_w=gate_w_b, gate_b=gate_b,
                                 w_br=w_br_b, w_o=w_o_b, ffn_up=ffn_up_b, ffn_down=ffn_down_b)
        xl = tail(xl, mod_row=None, ya=ya_l, yb=yb_l, yc=yc_l)

        if not last:
            ya_c = _attention(qc, (kc,), (vc,), da_qk_gain[l], da_lambda[l], da_subln[l], l)
            yb_c = _hyena(hy_c, hy_conv_w[l], hy_conv_b[l], filt_params, hy_skip[l], tables_c)
            xc = tail(xc, mod_row=ctx_row, ya=ya_c, yb=yb_c, yc=yc_c)
    return xl
```

```python
import functools
import math

import numpy as np
import jax
import jax.numpy as jnp
from jax import lax
from jax.experimental import pallas as pl
from jax.experimental.pallas import tpu as pltpu

F32 = jnp.float32
BF16 = jnp.bfloat16

D_MODEL = 1024
DEPTH = 2
GRID_W = 64
EPS = 1e-6
N_MOD = 9
FFN_HIDDEN = 2816
FFN_CHUNKS = ((0, 1536), (1536, FFN_HIDDEN))

DA_HEADS = 4
DA_DH = 64
DA_DV = 2 * DA_DH
DA_WIDTH = DA_HEADS * DA_DV
ROPE_BASE = 10000.0
ROPE_PAIRS = DA_DH // 4

HY_WIDTH = 256
HY_ORDER = 2
HY_EMB = 33
HY_BANDS = (HY_EMB - 1) // 2
HY_FAST_DECAY = 0.3
HY_SLOW_DECAY = 1.5
HY_TARGET = 1e-2

SG_WIDTH = 256
SG_GROUPS = 4
SG_CHUNK = 128

HY_OFF = 3 * DA_WIDTH
SG_OFF = HY_OFF + (HY_ORDER + 1) * HY_WIDTH
IN_WIDTH = SG_OFF + 2 * SG_WIDTH

VMEM_LIMIT_BYTES = 56 * 1024 * 1024
LONG_CONV_VMEM_BYTES = 60 * 1024 * 1024
LANES = 128
SUBLANES = 8
FFT_N2 = 128
MOD_ROWS = 16
Q_SCALE = math.log2(math.e) * DA_DH ** -0.5
ATTN_Q_TILE = 512
ATTN_KEY_BLOCK = 256
ATTN_VT_ROWS = DA_DV + 16
ATTN_BOUND_MARGIN = 1.02
ATTN_MIN_MASS = 2.0 ** -60
ATTN_M_INIT = -1e30


def _cparams(*sem, vmem=VMEM_LIMIT_BYTES):
    return pltpu.CompilerParams(dimension_semantics=sem, vmem_limit_bytes=vmem)


def _resident(shape, index_map):
    return pl.BlockSpec(shape, index_map, pipeline_mode=pl.Buffered(1))


def _dot(a, b):
    return jnp.dot(a, b, preferred_element_type=F32)


def _split(a):
    hi = a.astype(BF16)
    lo = (a - hi.astype(F32)).astype(BF16)
    return hi, lo


def _dot3(a, b):
    ah, al = _split(a)
    bh, bl = _split(b)
    return _dot(ah, bh) + (_dot(al, bh) + _dot(ah, bl))


def _sigmoid(x):
    return 1.0 / (1.0 + jnp.exp(-x))


def _adaln(x, gain, shift, scale):
    y = x * lax.rsqrt(jnp.mean(x * x, axis=-1, keepdims=True) + EPS)
    return (y * gain) * (1.0 + scale) + shift


def _mod_kernel(c_ref, w_ref, b_ref, o_ref):
    c = c_ref[...]
    s = c * _sigmoid(c)
    o_ref[0] = _dot(s.astype(BF16), w_ref[0].astype(BF16)) + b_ref[0]


def _modulation(cc, ada_w, ada_b):
    depth, d, n = ada_w.shape
    tn = 1536
    return pl.pallas_call(
        _mod_kernel,
        grid=(depth, n // tn),
        in_specs=[
            pl.BlockSpec((MOD_ROWS, d), lambda l, j: (0, 0)),
            pl.BlockSpec((1, d, tn), lambda l, j: (l, 0, j)),
            pl.BlockSpec((1, 1, tn), lambda l, j: (l, 0, j)),
        ],
        out_specs=pl.BlockSpec((1, MOD_ROWS, tn), lambda l, j: (l, 0, j)),
        out_shape=jax.ShapeDtypeStruct((depth, MOD_ROWS, n), F32),
        compiler_params=_cparams("parallel", "parallel"),
        name="modulation",
    )(cc, ada_w, ada_b.reshape(depth, 1, n))


def _ffn_apply(x, mod_ref, sub, gain, wu_ref, wd_ref):
    shift = mod_ref[0, 3 * sub:3 * sub + 1, :]
    scale = mod_ref[0, 3 * sub + 1:3 * sub + 2, :]
    gate = mod_ref[0, 3 * sub + 2:3 * sub + 3, :]
    h = _adaln(x, gain, shift, scale).astype(BF16)
    acc = None
    for lo, hi in FFN_CHUNKS:
        a = _dot(h, wu_ref[:, lo:hi])
        b = _dot(h, wu_ref[:, FFN_HIDDEN + lo:FFN_HIDDEN + hi])
        part = _dot((a * _sigmoid(a) * b).astype(BF16), wd_ref[lo:hi, :])
        acc = part if acc is None else acc + part
    return x + (0.5 * gate) * acc


def _group_sums(sq, ones_bd):
    outs = []
    for j in range(sq.shape[1] // 256):
        hi, lo = _split(sq[:, 256 * j:256 * (j + 1)])
        outs.append(_dot(hi, ones_bd) + _dot(lo, ones_bd))
    return jnp.concatenate(outs, axis=-1)


def _swap16(x):
    n = x.shape[-1]
    lane = lax.broadcasted_iota(jnp.int32, x.shape, x.ndim - 1)
    up = pltpu.roll(x, 16, x.ndim - 1)
    dn = pltpu.roll(x, n - 16, x.ndim - 1)
    return jnp.where((lane & 16) != 0, up, dn)


def _ffn_mixer_kernel(x_ref, mod_ref, g_ref, wu_ref, wd_ref, w_ref, qkg_ref, cos_ref, sin_ref, ones_ref,
                      lng_ref, lnb_ref, ws_ref, bs_ref,
                      x1_ref, q_ref, k_ref, vt_ref, hy_ref, yc_ref, *, rope):
    x = _ffn_apply(x_ref[0], mod_ref, 0, g_ref[0:1, :], wu_ref, wd_ref)
    x1_ref[0] = x
    tm = x.shape[0]
    h = _adaln(x, g_ref[1:2, :], mod_ref[0, 3:4, :], mod_ref[0, 4:5, :]).astype(BF16)

    qk = _dot(h, w_ref[:, :2 * DA_WIDTH])
    v = _dot(h, w_ref[:, 2 * DA_WIDTH:HY_OFF])
    hy_ref[0] = _dot(h, w_ref[:, HY_OFF:SG_OFF])
    sg = _dot(h, w_ref[:, SG_OFF:])

    ss = _group_sums(qk * qk, ones_ref[...])
    qk = qk * lax.rsqrt(ss * (1.0 / DA_DH) + EPS) * qkg_ref[...]
    if rope:
        reps = 2 * DA_WIDTH // LANES
        cosf = jnp.concatenate([cos_ref[...]] * reps, axis=-1)
        sinf = jnp.concatenate([sin_ref[...]] * reps, axis=-1)
        qk = qk * cosf + _swap16(qk) * sinf
    q_ref[0] = (qk[:, :DA_WIDTH] * Q_SCALE).astype(BF16)
    k_ref[0] = qk[:, DA_WIDTH:].astype(BF16)

    fill = (lax.broadcasted_iota(jnp.int32, (ATTN_VT_ROWS - DA_DV, tm), 0) == 0).astype(BF16)
    for hd in range(DA_HEADS):
        vt_ref[0, hd * ATTN_VT_ROWS:hd * ATTN_VT_ROWS + DA_DV, :] = v[:, hd * DA_DV:(hd + 1) * DA_DV].T.astype(BF16)
        vt_ref[0, hd * ATTN_VT_ROWS + DA_DV:(hd + 1) * ATTN_VT_ROWS, :] = fill

    sg = 0.5 * sg * (1.0 + lax.erf(sg * (2.0 ** -0.5)))
    u = sg[:, :SG_WIDTH]
    vv = sg[:, SG_WIDTH:]
    mu = jnp.mean(vv, axis=-1, keepdims=True)
    var = jnp.mean(jnp.square(vv - mu), axis=-1, keepdims=True)
    vv = ((vv - mu) * lax.rsqrt(var + EPS) * lng_ref[...] + lnb_ref[...]).astype(BF16)
    group = lax.broadcasted_iota(jnp.int32, (SG_CHUNK, SG_WIDTH), 1) // (SG_WIDTH // SG_GROUPS)
    for c in range(tm // SG_CHUNK):
        rows = slice(c * SG_CHUNK, (c + 1) * SG_CHUNK)
        mixed = _dot(ws_ref[...], vv[rows])
        sel = mixed[:SG_CHUNK]
        for g in range(1, SG_GROUPS):
            sel = jnp.where(group == g, mixed[g * SG_CHUNK:(g + 1) * SG_CHUNK], sel)
        yc_ref[0, rows, :] = (u[rows] * (sel + bs_ref[...])).astype(BF16)


def _layer_spec(arr, layer, *lead):
    idx = (layer,) + lead
    return _resident((None,) * len(idx) + arr.shape[len(idx):], lambda i, j: idx + (0, 0))


def _mod_spec(layer, row):
    d = D_MODEL
    if row is None:
        return pl.BlockSpec((None, 1, N_MOD, d), lambda i, j: (layer, i, 0, 0))
    return pl.BlockSpec((None, 1, N_MOD, d), lambda i, j: (layer, row, 0, 0))


def _ffn_mixer(x, mod, mod_row, layer, norm_g, ffn_up, ffn_down, w_in, qk_gain_row, cos_t, sin_t, ones_bd,
               ln_g, ln_b, ws_stack, bs_tile, rope):
    b, s, d = x.shape
    tm = min(s, 512)
    const = lambda i, j: (0, 0)
    tok = lambda i, j: (i, j, 0)
    hy_w = SG_OFF - HY_OFF
    return pl.pallas_call(
        functools.partial(_ffn_mixer_kernel, rope=rope),
        grid=(b, s // tm),
        in_specs=[
            pl.BlockSpec((1, tm, d), tok),
            _mod_spec(layer, mod_row),
            pl.BlockSpec((None, 3, d), lambda i, j: (layer, 0, 0)),
            _layer_spec(ffn_up, layer, 0),
            _layer_spec(ffn_down, layer, 0),
            _layer_spec(w_in, layer),
            pl.BlockSpec((1, 2 * DA_WIDTH), const),
            pl.BlockSpec((tm, LANES), lambda i, j: (j, 0)),
            pl.BlockSpec((tm, LANES), lambda i, j: (j, 0)),
            pl.BlockSpec((256, 256), const),
            pl.BlockSpec((1, SG_WIDTH), const),
            pl.BlockSpec((1, SG_WIDTH), const),
            pl.BlockSpec((SG_GROUPS * SG_CHUNK, SG_CHUNK), const),
            pl.BlockSpec((SG_CHUNK, SG_WIDTH), const),
        ],
        out_specs=[
            pl.BlockSpec((1, tm, d), tok),
            pl.BlockSpec((1, tm, DA_WIDTH), tok),
            pl.BlockSpec((1, tm, DA_WIDTH), tok),
            pl.BlockSpec((1, DA_HEADS * ATTN_VT_ROWS, tm), lambda i, j: (i, 0, j)),
            pl.BlockSpec((1, tm, hy_w), tok),
            pl.BlockSpec((1, tm, SG_WIDTH), tok),
        ],
        out_shape=[
            jax.ShapeDtypeStruct((b, s, d), F32),
            jax.ShapeDtypeStruct((b, s, DA_WIDTH), BF16),
            jax.ShapeDtypeStruct((b, s, DA_WIDTH), BF16),
            jax.ShapeDtypeStruct((b, DA_HEADS * ATTN_VT_ROWS, s), BF16),
            jax.ShapeDtypeStruct((b, s, hy_w), F32),
            jax.ShapeDtypeStruct((b, s, SG_WIDTH), BF16),
        ],
        compiler_params=_cparams("parallel", "parallel"),
        name="ffn_mixer_in_rope" if rope else "ffn_mixer_in",
    )(x, mod, norm_g, ffn_up, ffn_down, w_in, qk_gain_row, cos_t, sin_t, ones_bd,
      ln_g.reshape(1, -1), ln_b.reshape(1, -1), ws_stack, bs_tile)


def _attn_kernel(*refs, lam_init, n_parts):
    bound_ref, q_ref = refs[:2]
    k_refs = refs[2:2 + n_parts]
    vt_refs = refs[2 + n_parts:2 + 2 * n_parts]
    lv_ref, sub_ref, o_ref, p_ref, acc_ref = refs[2 + 2 * n_parts:]
    lv = lv_ref[...]
    lam = (jnp.exp(jnp.sum(lv[0:1] * lv[1:2], axis=-1, keepdims=True))
           - jnp.exp(jnp.sum(lv[2:3] * lv[3:4], axis=-1, keepdims=True)) + lam_init)
    q = q_ref[0]
    tq = q.shape[0]
    nt = (((1,), (1,)), ((), ()))
    bound = bound_ref[...]
    sizes = [k_ref.shape[1] for k_ref in k_refs]
    bases = [sum(sizes[:i]) for i in range(n_parts)]

    def block_scores(k_ref, start, c):
        cols = slice(c * DA_DH, (c + 1) * DA_DH)
        kblk = k_ref[0, pl.ds(start, ATTN_KEY_BLOCK), cols]
        return lax.dot_general(kblk, q[:, cols], nt, preferred_element_type=F32)

    for k_ref, base, size in zip(k_refs, bases, sizes):
        for start in range(0, size, ATTN_KEY_BLOCK):
            for c in range(2):
                s = block_scores(k_ref, start, c)
                p_ref[c, base + start:base + start + ATTN_KEY_BLOCK, :] = jnp.exp2(s - bound).astype(BF16)
    for c in range(2):
        acc = None
        for vt_ref, base, size in zip(vt_refs, bases, sizes):
            part = _dot(vt_ref[0], p_ref[c, base:base + size, :])
            acc = part if acc is None else acc + part
        acc_ref[c] = acc
    mass = jnp.minimum(acc_ref[0, DA_DV:DA_DV + 1, :], acc_ref[1, DA_DV:DA_DV + 1, :])

    @pl.when(jnp.min(mass) < ATTN_MIN_MASS)
    def _():
        acc_ref[...] = jnp.zeros(acc_ref.shape, F32)
        maxima = (jnp.full((1, tq), ATTN_M_INIT, F32),) * 2
        for k_ref, vt_ref, size in zip(k_refs, vt_refs, sizes):
            def slow_block(j, maxima, k_ref=k_ref, vt_ref=vt_ref):
                start = pl.multiple_of(j * ATTN_KEY_BLOCK, ATTN_KEY_BLOCK)
                out = []
                for c in range(2):
                    s = block_scores(k_ref, start, c)
                    m_new = jnp.maximum(maxima[c], jnp.max(s, axis=0, keepdims=True))
                    vblk = vt_ref[0, :, pl.ds(start, ATTN_KEY_BLOCK)]
                    acc_ref[c] = (jnp.exp2(maxima[c] - m_new) * acc_ref[c]
                                  + _dot(vblk, jnp.exp2(s - m_new).astype(BF16)))
                    out.append(m_new)
                return tuple(out)

            maxima = lax.fori_loop(0, size // ATTN_KEY_BLOCK, slow_block, maxima)

    l0 = acc_ref[0, DA_DV:DA_DV + 1, :]
    l1 = acc_ref[1, DA_DV:DA_DV + 1, :]
    o = acc_ref[0, :DA_DV, :] * (1.0 / l0) - acc_ref[1, :DA_DV, :] * (lam / l1)
    y = o * lax.rsqrt(jnp.mean(o * o, axis=0, keepdims=True) + EPS)
    sub = jnp.concatenate([sub_ref[...]] * (tq // LANES), axis=-1)
    o_ref[0] = (y * sub * (1.0 - lam_init)).T.astype(BF16)


def _attention(q, k_parts, vt_parts, qk_gain, lam_vecs, subln, layer_idx):
    b, sq, _ = q.shape
    sk = sum(k.shape[1] for k in k_parts)
    tq = min(sq, ATTN_Q_TILE)
    lam_init = 0.8 - 0.6 * math.exp(-0.3 * layer_idx)
    bound = (ATTN_BOUND_MARGIN * Q_SCALE * DA_DH) * jnp.max(jnp.abs(qk_gain[0])) * jnp.max(jnp.abs(qk_gain[1]))
    return pl.pallas_call(
        functools.partial(_attn_kernel, lam_init=lam_init, n_parts=len(k_parts)),
        grid=(b, DA_HEADS, sq // tq),
        in_specs=(
            [pl.BlockSpec((1, 1), lambda i, h, j: (0, 0)),
             pl.BlockSpec((1, tq, DA_DV), lambda i, h, j: (i, j, h))]
            + [pl.BlockSpec((1, k.shape[1], DA_DV), lambda i, h, j: (i, 0, h)) for k in k_parts]
            + [pl.BlockSpec((1, ATTN_VT_ROWS, vt.shape[2]), lambda i, h, j: (i, h, 0)) for vt in vt_parts]
            + [pl.BlockSpec((4, DA_DH), lambda i, h, j: (0, 0)),
               pl.BlockSpec((DA_DV, LANES), lambda i, h, j: (0, 0))]),
        out_specs=pl.BlockSpec((1, tq, DA_DV), lambda i, h, j: (i, j, h)),
        out_shape=jax.ShapeDtypeStruct((b, sq, DA_WIDTH), BF16),
        scratch_shapes=[pltpu.VMEM((2, sk, tq), BF16),
                        pltpu.VMEM((2, ATTN_VT_ROWS, tq), F32)],
        compiler_params=_cparams("parallel", "parallel", "parallel"),
        name="diff_attention",
    )(bound.reshape(1, 1), q, *k_parts, *vt_parts, lam_vecs, jnp.broadcast_to(subln[:, None], (DA_DV, LANES)))


def _short_conv_kernel(p0_ref, p1_ref, p2_ref, w0_ref, w1_ref, w2_ref, b0_ref, b1_ref, b2_ref,
                       o0_ref, o1_ref, o2_ref):
    n = p0_ref.shape[1]
    row = lax.broadcasted_iota(jnp.int32, (n, LANES), 0)
    for p_ref, w_ref, b_ref, o_ref in ((p0_ref, w0_ref, b0_ref, o0_ref),
                                       (p1_ref, w1_ref, b1_ref, o1_ref),
                                       (p2_ref, w2_ref, b2_ref, o2_ref)):
        p = p_ref[0]
        prev = jnp.where(row == 0, 0.0, pltpu.roll(p, 1, 0))
        nxt = jnp.where(row == n - 1, 0.0, pltpu.roll(p, n - 1, 0))
        w = w_ref[...]
        o_ref[0] = prev * w[0:1] + p * w[1:2] + nxt * w[2:3] + b_ref[...]


def _short_conv(p, conv_w, conv_b):
    b, n, _ = p.shape
    halves = HY_WIDTH // LANES
    seq = lambda part: pl.BlockSpec((1, n, LANES), lambda i, j: (i, 0, part * halves + j))
    wsp = lambda part: pl.BlockSpec((3, LANES), lambda i, j: (0, part * halves + j))
    bsp = lambda part: pl.BlockSpec((1, LANES), lambda i, j: (0, part * halves + j))
    out = pl.BlockSpec((1, n, LANES), lambda i, j: (i, 0, j))
    shp = jax.ShapeDtypeStruct((b, n, HY_WIDTH), F32)
    cb = conv_b.reshape(1, -1)
    return pl.pallas_call(
        _short_conv_kernel,
        grid=(b, halves),
        in_specs=[seq(0), seq(1), seq(2), wsp(0), wsp(1), wsp(2), bsp(0), bsp(1), bsp(2)],
        out_specs=[out, out, out],
        out_shape=[shp, shp, shp],
        compiler_params=_cparams("parallel", "parallel"),
        name="hyena_short_conv",
    )(p, p, p, conv_w, conv_w, conv_w, cb, cb, cb)


def _filter_kernel(z_ref, dec_ref, w1_ref, b1_ref, w2_ref, b2_ref, fr_ref, w3_ref, f_ref, l1_ref):
    i = pl.program_id(0)
    fr = fr_ref[...]
    h = jnp.sin(fr * (_dot3(z_ref[...], w1_ref[...]) + b1_ref[...]))
    h = jnp.sin(fr * (_dot3(h, w2_ref[...]) + b2_ref[...]))
    h = _dot3(h, w3_ref[...])
    tr = h.shape[0]
    half = HY_ORDER * HY_WIDTH
    dec = dec_ref[...]
    fwd = h[:, :half] * dec
    first = (lax.broadcasted_iota(jnp.int32, (tr, half), 0) + i * tr) == 0
    bwd = jnp.where(first, 0.0, h[:, half:] * dec)
    for o in range(HY_ORDER):
        f_ref[o] = fwd[:, o * HY_WIDTH:(o + 1) * HY_WIDTH]
        f_ref[HY_ORDER + o] = bwd[:, o * HY_WIDTH:(o + 1) * HY_WIDTH]
    part = jnp.sum(jnp.abs(fwd) + jnp.abs(bwd), axis=0, keepdims=True)

    @pl.when(i == 0)
    def _():
        l1_ref[...] = part

    @pl.when(i != 0)
    def _():
        l1_ref[...] += part


def _hyena_filter_taps(n, w1, b1, w2, b2, freq, w3):
    t = jnp.linspace(0.0, 1.0, n, dtype=F32)[:, None]
    w = (2.0 * math.pi / n) * jnp.arange(n, dtype=F32)[:, None]
    bands = jnp.linspace(1e-4, HY_BANDS - 1, HY_BANDS, dtype=F32)
    z = jnp.concatenate([t, jnp.cos(bands * w), -jnp.sin(bands * w)], axis=-1)
    emb = 48
    z = jnp.pad(z, ((0, 0), (0, emb - HY_EMB)))
    w1p = jnp.pad(w1, ((0, emb - HY_EMB), (0, 0)))
    max_decay = math.log(HY_TARGET) / HY_FAST_DECAY
    min_decay = math.log(HY_TARGET) / HY_SLOW_DECAY
    deltas = jnp.linspace(min_decay, max_decay, HY_ORDER * HY_WIDTH, dtype=F32)
    decay = jnp.exp(-t * jnp.abs(deltas)[None, :])
    tr = min(n, 512)
    hid = w2.shape[0]
    half = HY_ORDER * HY_WIDTH
    const = lambda i: (0, 0)
    return pl.pallas_call(
        _filter_kernel,
        grid=(n // tr,),
        in_specs=[
            pl.BlockSpec((tr, emb), lambda i: (i, 0)),
            pl.BlockSpec((tr, half), lambda i: (i, 0)),
            pl.BlockSpec((emb, hid), const),
            pl.BlockSpec((1, hid), const),
            pl.BlockSpec((hid, hid), const),
            pl.BlockSpec((1, hid), const),
            pl.BlockSpec((1, hid), const),
            pl.BlockSpec((hid, 2 * half), const),
        ],
        out_specs=[
            pl.BlockSpec((2 * HY_ORDER, tr, HY_WIDTH), lambda i: (0, i, 0)),
            pl.BlockSpec((1, half), const),
        ],
        out_shape=[
            jax.ShapeDtypeStruct((2 * HY_ORDER, n, HY_WIDTH), F32),
            jax.ShapeDtypeStruct((1, half), F32),
        ],
        compiler_params=_cparams("arbitrary"),
        name="hyena_filter_taps",
    )(z, decay, w1p, b1.reshape(1, -1), w2, b2.reshape(1, -1), freq.reshape(1, -1), w3)


@functools.lru_cache(maxsize=None)
def _dft_tables(n1):
    n2 = FFT_N2
    n = n1 * n2
    half = n1 // 2
    k1 = np.arange(half + 1)[:, None].astype(np.float64)
    t1 = np.arange(half)[None, :].astype(np.float64)
    ang_a = 2.0 * np.pi * k1 * t1 / n1
    eye = np.eye(SUBLANES)
    wa_fwd = np.kron(np.concatenate([np.cos(ang_a), -np.sin(ang_a)], axis=0), eye)
    weight = np.where(k1[:half] == 0, 1.0, 2.0)
    wa_inv = np.kron(np.concatenate([(weight * np.cos(ang_a[:half])).T,
                                     (-weight * np.sin(ang_a[:half])).T], axis=1), eye)
    kk1 = np.arange(half + 1)[:, None, None].astype(np.float64)
    kk2 = np.arange(n2)[None, :, None].astype(np.float64)
    tt2 = np.arange(n2)[None, None, :].astype(np.float64)
    ang_b = 2.0 * np.pi * tt2 * (kk1 + n1 * kk2) / n
    er, ei = np.cos(ang_b), -np.sin(ang_b)
    mf = np.concatenate([np.concatenate([er, -ei], axis=2),
                         np.concatenate([ei, er], axis=2)], axis=1)
    mi = np.swapaxes(mf, 1, 2)
    f32 = lambda a: np.ascontiguousarray(a.astype(np.float32))
    return f32(wa_fwd), f32(wa_inv), f32(mf), f32(mi)


def _tile_rows(ref, g, count):
    start = pl.multiple_of(g * SUBLANES, SUBLANES)
    return jnp.concatenate([ref[pl.ds(t * FFT_N2 + start, SUBLANES), :] for t in range(count)], axis=0)


def _dft_forward(u_ref, a_ref, wa_ref, mf_ref, half, emit):
    ks = half + 1

    def stage_a(g, carry):
        start = pl.multiple_of(g * SUBLANES, SUBLANES)
        r = _dot(wa_ref[...], _tile_rows(u_ref, g, half).astype(BF16))
        for part in range(2):
            for k1 in range(ks):
                src = (part * ks + k1) * SUBLANES
                a_ref[part, pl.ds(k1 * FFT_N2 + start, SUBLANES), :] = r[src:src + SUBLANES]
        return carry

    lax.fori_loop(0, FFT_N2 // SUBLANES, stage_a, 0, unroll=2)

    group = 3 if ks % 3 == 0 else 1

    def stage_b(i, carry):
        base = pl.multiple_of(i * (group * FFT_N2), FFT_N2)
        rows = [pl.ds(base + j * FFT_N2, FFT_N2) for j in range(group)]
        xs = [_dot(mf_ref[i * group + j],
                   jnp.concatenate([a_ref[0, rows[j], :], a_ref[1, rows[j], :]], axis=0).astype(BF16))
              for j in range(group)]
        emit(i * group, rows, xs)
        return carry

    lax.fori_loop(0, ks // group, stage_b, 0)


def _spectrum_kernel(t_ref, wa_ref, mf_ref, l1_ref, h_ref, a_ref, *, half, scale):
    d = pl.program_id(1)
    inv = scale / l1_ref[0]

    def emit(k0, rows, xs):
        @pl.when(d == 0)
        def _():
            for j, x in enumerate(xs):
                h_ref[0, 0, k0 + j] = x[:FFT_N2] * inv
                h_ref[0, 1, k0 + j] = x[FFT_N2:] * inv

        @pl.when(d != 0)
        def _():
            for j, x in enumerate(xs):
                h_ref[0, 0, k0 + j] += x[:FFT_N2] * inv
                h_ref[0, 1, k0 + j] -= x[FFT_N2:] * inv

    _dft_forward(t_ref, a_ref, wa_ref, mf_ref, half, emit)


def _filter_spectrum(taps, wa_fwd, mf, l1, n1):
    _, n, c = taps.shape
    half = n1 // 2
    ks = half + 1
    return pl.pallas_call(
        functools.partial(_spectrum_kernel, half=half, scale=1.0 / (n1 * FFT_N2)),
        grid=(HY_ORDER, 2),
        in_specs=[pl.BlockSpec((None, n, c), lambda o, d: (d * HY_ORDER + o, 0, 0)),
                  pl.BlockSpec(wa_fwd.shape, lambda o, d: (0, 0)),
                  _resident(mf.shape, lambda o, d: (0, 0, 0)),
                  pl.BlockSpec((1, 1, c), lambda o, d: (o, 0, 0))],
        out_specs=pl.BlockSpec((1, 2, ks, FFT_N2, c), lambda o, d: (o, 0, 0, 0, 0)),
        out_shape=jax.ShapeDtypeStruct((HY_ORDER, 2, ks, FFT_N2, c), F32),
        scratch_shapes=[pltpu.VMEM((2, ks * FFT_N2, c), F32)],
        compiler_params=_cparams("parallel", "arbitrary"),
        name="hyena_filter_spectrum",
    )(taps, wa_fwd, mf, l1.reshape(HY_ORDER, 1, c))


def _long_conv_kernel(u_ref, g_ref, skip_ref, wa_ref, wi_ref, mf_ref, mi_ref, h_ref, o_ref, a_ref, *, half):
    u_ref, g_ref, o_ref = u_ref.at[0], g_ref.at[0], o_ref.at[0]

    def emit(k0, rows, xs):
        cs = []
        for j, x in enumerate(xs):
            xr, xi = x[:FFT_N2], x[FFT_N2:]
            hr, hi = h_ref[0, k0 + j], h_ref[1, k0 + j]
            y = jnp.concatenate([xr * hr - xi * hi, xr * hi + xi * hr], axis=0).astype(BF16)
            cs.append(_dot(mi_ref[k0 + j], y))
        for j, c in enumerate(cs):
            a_ref[0, rows[j], :] = c[:FFT_N2]
            a_ref[1, rows[j], :] = c[FFT_N2:]

    _dft_forward(u_ref, a_ref, wa_ref, mf_ref, half, emit)

    tile_rows = half * SUBLANES
    alt = (1 - 2 * ((lax.broadcasted_iota(jnp.int32, (tile_rows, 1), 0) // SUBLANES) & 1)).astype(F32)
    skip = skip_ref[...]

    def stage_a_inv(g, carry):
        start = pl.multiple_of(g * SUBLANES, SUBLANES)
        c = jnp.concatenate([_tile_rows(a_ref.at[0], g, half), _tile_rows(a_ref.at[1], g, half)], axis=0)
        nyq = a_ref[0, pl.ds(half * FFT_N2 + start, SUBLANES), :]
        y = _dot(wi_ref[...], c.astype(BF16)) + alt * jnp.concatenate([nyq] * half, axis=0)
        out = _tile_rows(g_ref, g, half) * (y + _tile_rows(u_ref, g, half) * skip)
        for t1 in range(half):
            o_ref[pl.ds(t1 * FFT_N2 + start, SUBLANES), :] = out[t1 * SUBLANES:(t1 + 1) * SUBLANES]
        return carry

    lax.fori_loop(0, FFT_N2 // SUBLANES, stage_a_inv, 0, unroll=2)


def _long_conv(u, gate, skip_row, tables, hspec, order, n1):
    b, n, c = u.shape
    half = n1 // 2
    ks = half + 1
    wa_fwd, wa_inv, mf, mi = tables
    seq = pl.BlockSpec((1, n, c), lambda i: (i, 0, 0))
    const2 = lambda i: (0, 0)
    const3 = lambda i: (0, 0, 0)
    return pl.pallas_call(
        functools.partial(_long_conv_kernel, half=half),
        grid=(b,),
        in_specs=[seq, seq,
                  pl.BlockSpec((1, c), const2),
                  pl.BlockSpec(wa_fwd.shape, const2),
                  pl.BlockSpec(wa_inv.shape, const2),
                  _resident(mf.shape, const3),
                  _resident(mi.shape, const3),
                  _resident((None, 2, ks, FFT_N2, c), lambda i: (order, 0, 0, 0, 0))],
        out_specs=seq,
        out_shape=jax.ShapeDtypeStruct(u.shape, F32),
        scratch_shapes=[pltpu.VMEM((2, ks * FFT_N2, c), F32)],
        compiler_params=_cparams("parallel", vmem=LONG_CONV_VMEM_BYTES),
        name="hyena_long_conv",
    )(u, gate, skip_row, wa_fwd, wa_inv, mf, mi, hspec)


def _hyena(p, conv_w, conv_b, filt_params, skip, tables):
    b, n, _ = p.shape
    n1 = 2 * n // FFT_N2
    wa_fwd, _, mf, _ = tables
    v, x1, x2 = _short_conv(p, conv_w, conv_b)
    taps, l1 = _hyena_filter_taps(n, *filt_params)
    hspec = _filter_spectrum(taps, wa_fwd, mf, l1, n1)
    z = _long_conv(v, x1, skip[0:1], tables, hspec, 0, n1)
    return _long_conv(z, x2, skip[1:2], tables, hspec, 1, n1)


def _merge_ffn_kernel(x_ref, mod_ref, g_ref, ya_ref, yb_ref, yc_ref, gw_ref, gb_ref, wbr_ref, wo_ref,
                      wu_ref, wd_ref, o_ref):
    x = x_ref[0]
    d = x.shape[1]
    h = _adaln(x, g_ref[1:2, :], mod_ref[0, 3:4, :], mod_ref[0, 4:5, :]).astype(BF16)
    branches = ((ya_ref[0], 0, DA_WIDTH),
                (yb_ref[0].astype(BF16), DA_WIDTH, DA_WIDTH + HY_WIDTH),
                (yc_ref[0], DA_WIDTH + HY_WIDTH, DA_WIDTH + HY_WIDTH + SG_WIDTH))
    m = None
    for i, (y, lo, hi) in enumerate(branches):
        gate = _sigmoid(_dot(h, gw_ref[:, i * d:(i + 1) * d]) + gb_ref[:, i * d:(i + 1) * d])
        part = gate * _dot(y, wbr_ref[lo:hi, :])
        m = part if m is None else m + part
    x = x + mod_ref[0, 5:6, :] * _dot(m.astype(BF16), wo_ref[...])
    o_ref[0] = _ffn_apply(x, mod_ref, 2, g_ref[2:3, :], wu_ref, wd_ref)


def _merge_ffn(x, mod, mod_row, layer, norm_g, ya, yb, yc, gate_w, gate_b, w_br, w_o, ffn_up, ffn_down):
    b, s, d = x.shape
    tm = min(s, 512)
    tok = lambda i, j: (i, j, 0)
    return pl.pallas_call(
        _merge_ffn_kernel,
        grid=(b, s // tm),
        in_specs=[
            pl.BlockSpec((1, tm, d), tok),
            _mod_spec(layer, mod_row),
            pl.BlockSpec((None, 3, d), lambda i, j: (layer, 0, 0)),
            pl.BlockSpec((1, tm, DA_WIDTH), tok),
            pl.BlockSpec((1, tm, HY_WIDTH), tok),
            pl.BlockSpec((1, tm, SG_WIDTH), tok),
            _layer_spec(gate_w, layer),
            pl.BlockSpec((None, 1, gate_w.shape[2]), lambda i, j: (layer, 0, 0)),
            _layer_spec(w_br, layer),
            _layer_spec(w_o, layer),
            _layer_spec(ffn_up, layer, 1),
            _layer_spec(ffn_down, layer, 1),
        ],
        out_specs=pl.BlockSpec((1, tm, d), tok),
        out_shape=jax.ShapeDtypeStruct((b, s, d), F32),
        compiler_params=_cparams("parallel", "parallel"),
        name="merge_ffn",
    )(x, mod, norm_g, ya, yb, yc, gate_w, gate_b.reshape(gate_b.shape[0], 1, -1), w_br, w_o, ffn_up, ffn_down)


def _rope_tables(n_tokens):
    t = np.arange(n_tokens)
    pos = np.stack([t // GRID_W, t % GRID_W], axis=-1).astype(np.float64)
    inv = ROPE_BASE ** (-np.arange(ROPE_PAIRS, dtype=np.float64) / ROPE_PAIRS)
    ang = pos[:, :, None] * inv
    cos, sin = np.cos(ang), np.sin(ang)
    cos64 = np.concatenate([cos[:, 0], cos[:, 0], cos[:, 1], cos[:, 1]], axis=-1)
    sin64 = np.concatenate([-sin[:, 0], sin[:, 0], -sin[:, 1], sin[:, 1]], axis=-1)
    as_f32 = lambda a: np.tile(a, (1, LANES // DA_DH)).astype(np.float32)
    return as_f32(cos64), as_f32(sin64)


def _device_tables(n1):
    return tuple(jnp.asarray(t).astype(BF16) for t in _dft_tables(n1))


def kernel(x, c, ctx, c_ctx, ada_w, ada_b, norm_g, ffn_up, ffn_down, w_in, da_qk_gain, da_lambda, da_subln, hy_conv_w, hy_conv_b, hy_f_w1, hy_f_b1, hy_f_w2, hy_f_b2, hy_f_freq, hy_f_w3, hy_skip, sg_ln_g, sg_ln_b, sg_w, sg_b, gate_w, gate_b, w_br, w_o):
    batch, seq, d = x.shape
    cos_t, sin_t = _rope_tables(seq)
    ones_bd = np.kron(np.eye(256 // DA_DH), np.ones((DA_DH, DA_DH))).astype(BF16)
    tables_l = _device_tables(2 * seq // FFT_N2)
    tables_c = _device_tables(2 * ctx.shape[1] // FFT_N2)

    cc = jnp.concatenate([c, c_ctx[None], jnp.zeros((MOD_ROWS - batch - 1, d), F32)], axis=0)
    mod = _modulation(cc, ada_w, ada_b).reshape(DEPTH, MOD_ROWS, N_MOD, d)

    ffn_up_b = ffn_up.astype(BF16)
    ffn_down_b = ffn_down.astype(BF16)
    w_in_b = w_in.astype(BF16)
    gate_w_b = gate_w.astype(BF16)
    w_br_b = w_br.astype(BF16)
    w_o_b = w_o.astype(BF16)
    sg_w_b = sg_w.astype(BF16)

    xl, xc = x, ctx
    ctx_row = batch
    for l in range(DEPTH):
        last = l == DEPTH - 1
        filt_params = (hy_f_w1[l], hy_f_b1[l], hy_f_w2[l], hy_f_b2[l], hy_f_freq[l], hy_f_w3[l])
        qk_gain_row = jnp.concatenate([jnp.tile(da_qk_gain[l, 0], 2 * DA_HEADS),
                                       jnp.tile(da_qk_gain[l, 1], 2 * DA_HEADS)]).reshape(1, -1)
        ws_stack = sg_w_b[l].reshape(SG_GROUPS * SG_CHUNK, SG_CHUNK)
        bs_tile = jnp.repeat(sg_b[l].T, SG_WIDTH // SG_GROUPS, axis=1)

        head = functools.partial(_ffn_mixer, mod=mod, layer=l, norm_g=norm_g, ffn_up=ffn_up_b, ffn_down=ffn_down_b,
                                 w_in=w_in_b, qk_gain_row=qk_gain_row, cos_t=cos_t, sin_t=sin_t, ones_bd=ones_bd,
                                 ln_g=sg_ln_g[l], ln_b=sg_ln_b[l], ws_stack=ws_stack, bs_tile=bs_tile)
        xl, ql, kl, vl, hy_l, yc_l = head(xl, mod_row=None, rope=True)
        xc, qc, kc, vc, hy_c, yc_c = head(xc, mod_row=ctx_row, rope=False)

        ya_l = _attention(ql, (kl, kc), (vl, vc), da_qk_gain[l], da_lambda[l], da_subln[l], l)
        yb_l = _hyena(hy_l, hy_conv_w[l], hy_conv_b[l], filt_params, hy_skip[l], tables_l)
        tail = functools.partial(_merge_ffn, mod=mod, layer=l, norm_g=norm_g, gate_w=gate_w_b, gate_b=gate_b,
                                 w_br=w_br_b, w_o=w_o_b, ffn_up=ffn_up_b, ffn_down=ffn_down_b)
        xl = tail(xl, mod_row=None, ya=ya_l, yb=yb_l, yc=yc_l)

        if not last:
            ya_c = _attention(qc, (kc,), (vc,), da_qk_gain[l], da_lambda[l], da_subln[l], l)
            yb_c = _hyena(hy_c, hy_conv_w[l], hy_conv_b[l], filt_params, hy_skip[l], tables_c)
            xc = tail(xc, mod_row=ctx_row, ya=ya_c, yb=yb_c, yc=yc_c)
    return xl
```

```python
import functools
import math

import numpy as np
import jax
import jax.numpy as jnp
from jax import lax
from jax.experimental import pallas as pl
from jax.experimental.pallas import tpu as pltpu

F32 = jnp.float32
BF16 = jnp.bfloat16

D_MODEL = 1024
DEPTH = 2
GRID_W = 64
EPS = 1e-6
N_MOD = 9
FFN_HIDDEN = 2816
FFN_CHUNKS = ((0, 1536), (1536, FFN_HIDDEN))

DA_HEADS = 4
DA_DH = 64
DA_DV = 2 * DA_DH
DA_WIDTH = DA_HEADS * DA_DV
ROPE_BASE = 10000.0
ROPE_PAIRS = DA_DH // 4

HY_WIDTH = 256
HY_ORDER = 2
HY_EMB = 33
HY_BANDS = (HY_EMB - 1) // 2
HY_FAST_DECAY = 0.3
HY_SLOW_DECAY = 1.5
HY_TARGET = 1e-2

SG_WIDTH = 256
SG_GROUPS = 4
SG_CHUNK = 128

HY_OFF = 3 * DA_WIDTH
SG_OFF = HY_OFF + (HY_ORDER + 1) * HY_WIDTH
IN_WIDTH = SG_OFF + 2 * SG_WIDTH

VMEM_LIMIT_BYTES = 56 * 1024 * 1024
LONG_CONV_VMEM_BYTES = 60 * 1024 * 1024
LANES = 128
SUBLANES = 8
FFT_N2 = 128
MOD_ROWS = 16
Q_SCALE = math.log2(math.e) * DA_DH ** -0.5
ATTN_Q_TILE = 512
ATTN_KEY_BLOCK = 256
ATTN_VT_ROWS = DA_DV + 16
ATTN_BOUND_MARGIN = 1.02
ATTN_MIN_MASS = 2.0 ** -60
ATTN_M_INIT = -1e30


def _cparams(*sem, vmem=VMEM_LIMIT_BYTES):
    return pltpu.CompilerParams(dimension_semantics=sem, vmem_limit_bytes=vmem)


def _resident(shape, index_map):
    return pl.BlockSpec(shape, index_map, pipeline_mode=pl.Buffered(1))


def _dot(a, b):
    return jnp.dot(a, b, preferred_element_type=F32)


def _split(a):
    hi = a.astype(BF16)
    lo = (a - hi.astype(F32)).astype(BF16)
    return hi, lo


def _dot3(a, b):
    ah, al = _split(a)
    bh, bl = _split(b)
    return _dot(ah, bh) + (_dot(al, bh) + _dot(ah, bl))


def _sigmoid(x):
    return 1.0 / (1.0 + jnp.exp(-x))


def _adaln(x, gain, shift, scale):
    y = x * lax.rsqrt(jnp.mean(x * x, axis=-1, keepdims=True) + EPS)
    return (y * gain) * (1.0 + scale) + shift


def _mod_kernel(c_ref, w_ref, b_ref, o_ref):
    c = c_ref[...]
    s = c * _sigmoid(c)
    o_ref[0] = _dot(s.astype(BF16), w_ref[0].astype(BF16)) + b_ref[0]


def _modulation(cc, ada_w, ada_b):
    depth, d, n = ada_w.shape
    tn = 1536
    return pl.pallas_call(
        _mod_kernel,
        grid=(depth, n // tn),
        in_specs=[
            pl.BlockSpec((MOD_ROWS, d), lambda l, j: (0, 0)),
            pl.BlockSpec((1, d, tn), lambda l, j: (l, 0, j)),
            pl.BlockSpec((1, 1, tn), lambda l, j: (l, 0, j)),
        ],
        out_specs=pl.BlockSpec((1, MOD_ROWS, tn), lambda l, j: (l, 0, j)),
        out_shape=jax.ShapeDtypeStruct((depth, MOD_ROWS, n), F32),
        compiler_params=_cparams("parallel", "parallel"),
        name="modulation",
    )(cc, ada_w, ada_b.reshape(depth, 1, n))


def _ffn_apply(x, mod_ref, sub, gain, wu_ref, wd_ref):
    shift = mod_ref[0, 3 * sub:3 * sub + 1, :]
    scale = mod_ref[0, 3 * sub + 1:3 * sub + 2, :]
    gate = mod_ref[0, 3 * sub + 2:3 * sub + 3, :]
    h = _adaln(x, gain, shift, scale).astype(BF16)
    acc = None
    for lo, hi in FFN_CHUNKS:
        a = _dot(h, wu_ref[:, lo:hi])
        b = _dot(h, wu_ref[:, FFN_HIDDEN + lo:FFN_HIDDEN + hi])
        part = _dot((a * _sigmoid(a) * b).astype(BF16), wd_ref[lo:hi, :])
        acc = part if acc is None else acc + part
    return x + (0.5 * gate) * acc


def _group_sums(sq, ones_bd):
    outs = []
    for j in range(sq.shape[1] // 256):
        hi, lo = _split(sq[:, 256 * j:256 * (j + 1)])
        outs.append(_dot(hi, ones_bd) + _dot(lo, ones_bd))
    return jnp.concatenate(outs, axis=-1)


def _swap16(x):
    n = x.shape[-1]
    lane = lax.broadcasted_iota(jnp.int32, x.shape, x.ndim - 1)
    up = pltpu.roll(x, 16, x.ndim - 1)
    dn = pltpu.roll(x, n - 16, x.ndim - 1)
    return jnp.where((lane & 16) != 0, up, dn)


def _ffn_mixer_kernel(x_ref, mod_ref, g_ref, wu_ref, wd_ref, w_ref, qkg_ref, cos_ref, sin_ref, ones_ref,
                      lng_ref, lnb_ref, ws_ref, bs_ref,
                      x1_ref, q_ref, k_ref, vt_ref, hy_ref, yc_ref, *, rope):
    x = _ffn_apply(x_ref[0], mod_ref, 0, g_ref[0:1, :], wu_ref, wd_ref)
    x1_ref[0] = x
    tm = x.shape[0]
    h = _adaln(x, g_ref[1:2, :], mod_ref[0, 3:4, :], mod_ref[0, 4:5, :]).astype(BF16)

    qk = _dot(h, w_ref[:, :2 * DA_WIDTH])
    v = _dot(h, w_ref[:, 2 * DA_WIDTH:HY_OFF])
    hy_ref[0] = _dot(h, w_ref[:, HY_OFF:SG_OFF])
    sg = _dot(h, w_ref[:, SG_OFF:])

    ss = _group_sums(qk * qk, ones_ref[...])
    qk = qk * lax.rsqrt(ss * (1.0 / DA_DH) + EPS) * qkg_ref[...]
    if rope:
        reps = 2 * DA_WIDTH // LANES
        cosf = jnp.concatenate([cos_ref[...]] * reps, axis=-1)
        sinf = jnp.concatenate([sin_ref[...]] * reps, axis=-1)
        qk = qk * cosf + _swap16(qk) * sinf
    q_ref[0] = (qk[:, :DA_WIDTH] * Q_SCALE).astype(BF16)
    k_ref[0] = qk[:, DA_WIDTH:].astype(BF16)

    fill = (lax.broadcasted_iota(jnp.int32, (ATTN_VT_ROWS - DA_DV, tm), 0) == 0).astype(BF16)
    for hd in range(DA_HEADS):
        vt_ref[0, hd * ATTN_VT_ROWS:hd * ATTN_VT_ROWS + DA_DV, :] = v[:, hd * DA_DV:(hd + 1) * DA_DV].T.astype(BF16)
        vt_ref[0, hd * ATTN_VT_ROWS + DA_DV:(hd + 1) * ATTN_VT_ROWS, :] = fill

    sg = 0.5 * sg * (1.0 + lax.erf(sg * (2.0 ** -0.5)))
    u = sg[:, :SG_WIDTH]
    vv = sg[:, SG_WIDTH:]
    mu = jnp.mean(vv, axis=-1, keepdims=True)
    var = jnp.mean(jnp.square(vv - mu), axis=-1, keepdims=True)
    vv = ((vv - mu) * lax.rsqrt(var + EPS) * lng_ref[...] + lnb_ref[...]).astype(BF16)
    group = lax.broadcasted_iota(jnp.int32, (SG_CHUNK, SG_WIDTH), 1) // (SG_WIDTH // SG_GROUPS)
    for c in range(tm // SG_CHUNK):
        rows = slice(c * SG_CHUNK, (c + 1) * SG_CHUNK)
        mixed = _dot(ws_ref[...], vv[rows])
        sel = mixed[:SG_CHUNK]
        for g in range(1, SG_GROUPS):
            sel = jnp.where(group == g, mixed[g * SG_CHUNK:(g + 1) * SG_CHUNK], sel)
        yc_ref[0, rows, :] = (u[rows] * (sel + bs_ref[...])).astype(BF16)


def _layer_spec(arr, layer, *lead):
    idx = (layer,) + lead
    return _resident((None,) * len(idx) + arr.shape[len(idx):], lambda i, j: idx + (0, 0))


def _mod_spec(layer, row):
    d = D_MODEL
    if row is None:
        return pl.BlockSpec((None, 1, N_MOD, d), lambda i, j: (layer, i, 0, 0))
    return pl.BlockSpec((None, 1, N_MOD, d), lambda i, j: (layer, row, 0, 0))


def _ffn_mixer(x, mod, mod_row, layer, norm_g, ffn_up, ffn_down, w_in, qk_gain_row, cos_t, sin_t, ones_bd,
               ln_g, ln_b, ws_stack, bs_tile, rope):
    b, s, d = x.shape
    tm = min(s, 512)
    const = lambda i, j: (0, 0)
    tok = lambda i, j: (i, j, 0)
    hy_w = SG_OFF - HY_OFF
    return pl.pallas_call(
        functools.partial(_ffn_mixer_kernel, rope=rope),
        grid=(b, s // tm),
        in_specs=[
            pl.BlockSpec((1, tm, d), tok),
            _mod_spec(layer, mod_row),
            pl.BlockSpec((None, 3, d), lambda i, j: (layer, 0, 0)),
            _layer_spec(ffn_up, layer, 0),
            _layer_spec(ffn_down, layer, 0),
            _layer_spec(w_in, layer),
            pl.BlockSpec((1, 2 * DA_WIDTH), const),
            pl.BlockSpec((tm, LANES), lambda i, j: (j, 0)),
            pl.BlockSpec((tm, LANES), lambda i, j: (j, 0)),
            pl.BlockSpec((256, 256), const),
            pl.BlockSpec((1, SG_WIDTH), const),
            pl.BlockSpec((1, SG_WIDTH), const),
            pl.BlockSpec((SG_GROUPS * SG_CHUNK, SG_CHUNK), const),
            pl.BlockSpec((SG_CHUNK, SG_WIDTH), const),
        ],
        out_specs=[
            pl.BlockSpec((1, tm, d), tok),
            pl.BlockSpec((1, tm, DA_WIDTH), tok),
            pl.BlockSpec((1, tm, DA_WIDTH), tok),
            pl.BlockSpec((1, DA_HEADS * ATTN_VT_ROWS, tm), lambda i, j: (i, 0, j)),
            pl.BlockSpec((1, tm, hy_w), tok),
            pl.BlockSpec((1, tm, SG_WIDTH), tok),
        ],
        out_shape=[
            jax.ShapeDtypeStruct((b, s, d), F32),
            jax.ShapeDtypeStruct((b, s, DA_WIDTH), BF16),
            jax.ShapeDtypeStruct((b, s, DA_WIDTH), BF16),
            jax.ShapeDtypeStruct((b, DA_HEADS * ATTN_VT_ROWS, s), BF16),
            jax.ShapeDtypeStruct((b, s, hy_w), F32),
            jax.ShapeDtypeStruct((b, s, SG_WIDTH), BF16),
        ],
        compiler_params=_cparams("parallel", "parallel"),
        name="ffn_mixer_in_rope" if rope else "ffn_mixer_in",
    )(x, mod, norm_g, ffn_up, ffn_down, w_in, qk_gain_row, cos_t, sin_t, ones_bd,
      ln_g.reshape(1, -1), ln_b.reshape(1, -1), ws_stack, bs_tile)


def _attn_kernel(*refs, lam_init, n_parts):
    bound_ref, q_ref = refs[:2]
    k_refs = refs[2:2 + n_parts]
    vt_refs = refs[2 + n_parts:2 + 2 * n_parts]
    lv_ref, sub_ref, o_ref, p_ref, acc_ref, o_scr = refs[2 + 2 * n_parts:]
    lv = lv_ref[...]
    lam = (jnp.exp(jnp.sum(lv[0:1] * lv[1:2], axis=-1, keepdims=True))
           - jnp.exp(jnp.sum(lv[2:3] * lv[3:4], axis=-1, keepdims=True)) + lam_init)
    q = q_ref[0]
    tq = q.shape[0]
    nt = (((1,), (1,)), ((), ()))
    bound = bound_ref[...]
    sizes = [k_ref.shape[1] for k_ref in k_refs]
    bases = [sum(sizes[:i]) for i in range(n_parts)]

    def block_scores(k_ref, start, c):
        cols = slice(c * DA_DH, (c + 1) * DA_DH)
        kblk = k_ref[0, pl.ds(start, ATTN_KEY_BLOCK), cols]
        return lax.dot_general(kblk, q[:, cols], nt, preferred_element_type=F32)

    mass = [None, None]
    for k_ref, base, size in zip(k_refs, bases, sizes):
        for start in range(0, size, ATTN_KEY_BLOCK):
            for c in range(2):
                p = jnp.exp2(block_scores(k_ref, start, c) - bound)
                part = jnp.sum(p.reshape(ATTN_KEY_BLOCK // SUBLANES, SUBLANES, tq), axis=0)
                mass[c] = part if mass[c] is None else mass[c] + part
                p_ref[c, base + start:base + start + ATTN_KEY_BLOCK, :] = p.astype(BF16)
    l0, l1 = (jnp.sum(m, axis=0, keepdims=True) for m in mass)
    ratio = (lam * l0 / l1).astype(BF16)
    o = None
    for vt_ref, base, size in zip(vt_refs, bases, sizes):
        for start in range(base, base + size, ATTN_KEY_BLOCK):
            rows = slice(start, start + ATTN_KEY_BLOCK)
            p_ref[0, rows, :] = p_ref[0, rows, :] - ratio * p_ref[1, rows, :]
        part = _dot(vt_ref[0, :DA_DV, :], p_ref[0, base:base + size, :])
        o = part if o is None else o + part
    o_scr[...] = o * (1.0 / l0)

    @pl.when(jnp.min(jnp.minimum(l0, l1)) < ATTN_MIN_MASS)
    def _():
        acc_ref[...] = jnp.zeros(acc_ref.shape, F32)
        maxima = (jnp.full((1, tq), ATTN_M_INIT, F32),) * 2
        for k_ref, vt_ref, size in zip(k_refs, vt_refs, sizes):
            def slow_block(j, maxima, k_ref=k_ref, vt_ref=vt_ref):
                start = pl.multiple_of(j * ATTN_KEY_BLOCK, ATTN_KEY_BLOCK)
                out = []
                for c in range(2):
                    s = block_scores(k_ref, start, c)
                    m_new = jnp.maximum(maxima[c], jnp.max(s, axis=0, keepdims=True))
                    vblk = vt_ref[0, :, pl.ds(start, ATTN_KEY_BLOCK)]
                    acc_ref[c] = (jnp.exp2(maxima[c] - m_new) * acc_ref[c]
                                  + _dot(vblk, jnp.exp2(s - m_new).astype(BF16)))
                    out.append(m_new)
                return tuple(out)

            maxima = lax.fori_loop(0, size // ATTN_KEY_BLOCK, slow_block, maxima)
        o_scr[...] = (acc_ref[0, :DA_DV, :] * (1.0 / acc_ref[0, DA_DV:DA_DV + 1, :])
                      - acc_ref[1, :DA_DV, :] * (lam / acc_ref[1, DA_DV:DA_DV + 1, :]))

    o = o_scr[...]
    y = o * lax.rsqrt(jnp.mean(o * o, axis=0, keepdims=True) + EPS)
    sub = jnp.concatenate([sub_ref[...]] * (tq // LANES), axis=-1)
    o_ref[0] = (y * sub * (1.0 - lam_init)).T.astype(BF16)


def _attention(q, k_parts, vt_parts, qk_gain, lam_vecs, subln, layer_idx):
    b, sq, _ = q.shape
    sk = sum(k.shape[1] for k in k_parts)
    tq = min(sq, ATTN_Q_TILE)
    lam_init = 0.8 - 0.6 * math.exp(-0.3 * layer_idx)
    bound = (ATTN_BOUND_MARGIN * Q_SCALE * DA_DH) * jnp.max(jnp.abs(qk_gain[0])) * jnp.max(jnp.abs(qk_gain[1]))
    return pl.pallas_call(
        functools.partial(_attn_kernel, lam_init=lam_init, n_parts=len(k_parts)),
        grid=(b, DA_HEADS, sq // tq),
        in_specs=(
            [pl.BlockSpec((1, 1), lambda i, h, j: (0, 0)),
             pl.BlockSpec((1, tq, DA_DV), lambda i, h, j: (i, j, h))]
            + [pl.BlockSpec((1, k.shape[1], DA_DV), lambda i, h, j: (i, 0, h)) for k in k_parts]
            + [pl.BlockSpec((1, ATTN_VT_ROWS, vt.shape[2]), lambda i, h, j: (i, h, 0)) for vt in vt_parts]
            + [pl.BlockSpec((4, DA_DH), lambda i, h, j: (0, 0)),
               pl.BlockSpec((DA_DV, LANES), lambda i, h, j: (0, 0))]),
        out_specs=pl.BlockSpec((1, tq, DA_DV), lambda i, h, j: (i, j, h)),
        out_shape=jax.ShapeDtypeStruct((b, sq, DA_WIDTH), BF16),
        scratch_shapes=[pltpu.VMEM((2, sk, tq), BF16),
                        pltpu.VMEM((2, ATTN_VT_ROWS, tq), F32),
                        pltpu.VMEM((DA_DV, tq), F32)],
        compiler_params=_cparams("parallel", "parallel", "parallel"),
        name="diff_attention",
    )(bound.reshape(1, 1), q, *k_parts, *vt_parts, lam_vecs, jnp.broadcast_to(subln[:, None], (DA_DV, LANES)))


def _short_conv_kernel(p0_ref, p1_ref, p2_ref, w0_ref, w1_ref, w2_ref, b0_ref, b1_ref, b2_ref,
                       o0_ref, o1_ref, o2_ref):
    n = p0_ref.shape[1]
    row = lax.broadcasted_iota(jnp.int32, (n, LANES), 0)
    for p_ref, w_ref, b_ref, o_ref in ((p0_ref, w0_ref, b0_ref, o0_ref),
                                       (p1_ref, w1_ref, b1_ref, o1_ref),
                                       (p2_ref, w2_ref, b2_ref, o2_ref)):
        p = p_ref[0]
        prev = jnp.where(row == 0, 0.0, pltpu.roll(p, 1, 0))
        nxt = jnp.where(row == n - 1, 0.0, pltpu.roll(p, n - 1, 0))
        w = w_ref[...]
        o_ref[0] = prev * w[0:1] + p * w[1:2] + nxt * w[2:3] + b_ref[...]


def _short_conv(p, conv_w, conv_b):
    b, n, _ = p.shape
    halves = HY_WIDTH // LANES
    seq = lambda part: pl.BlockSpec((1, n, LANES), lambda i, j: (i, 0, part * halves + j))
    wsp = lambda part: pl.BlockSpec((3, LANES), lambda i, j: (0, part * halves + j))
    bsp = lambda part: pl.BlockSpec((1, LANES), lambda i, j: (0, part * halves + j))
    out = pl.BlockSpec((1, n, LANES), lambda i, j: (i, 0, j))
    shp = jax.ShapeDtypeStruct((b, n, HY_WIDTH), F32)
    cb = conv_b.reshape(1, -1)
    return pl.pallas_call(
        _short_conv_kernel,
        grid=(b, halves),
        in_specs=[seq(0), seq(1), seq(2), wsp(0), wsp(1), wsp(2), bsp(0), bsp(1), bsp(2)],
        out_specs=[out, out, out],
        out_shape=[shp, shp, shp],
        compiler_params=_cparams("parallel", "parallel"),
        name="hyena_short_conv",
    )(p, p, p, conv_w, conv_w, conv_w, cb, cb, cb)


def _filter_kernel(z_ref, dec_ref, w1_ref, b1_ref, w2_ref, b2_ref, fr_ref, w3_ref, f_ref, l1_ref):
    i = pl.program_id(0)
    fr = fr_ref[...]
    h = jnp.sin(fr * (_dot3(z_ref[...], w1_ref[...]) + b1_ref[...]))
    h = jnp.sin(fr * (_dot3(h, w2_ref[...]) + b2_ref[...]))
    h = _dot3(h, w3_ref[...])
    tr = h.shape[0]
    half = HY_ORDER * HY_WIDTH
    dec = dec_ref[...]
    fwd = h[:, :half] * dec
    first = (lax.broadcasted_iota(jnp.int32, (tr, half), 0) + i * tr) == 0
    bwd = jnp.where(first, 0.0, h[:, half:] * dec)
    for o in range(HY_ORDER):
        f_ref[o] = fwd[:, o * HY_WIDTH:(o + 1) * HY_WIDTH]
        f_ref[HY_ORDER + o] = bwd[:, o * HY_WIDTH:(o + 1) * HY_WIDTH]
    part = jnp.sum(jnp.abs(fwd) + jnp.abs(bwd), axis=0, keepdims=True)

    @pl.when(i == 0)
    def _():
        l1_ref[...] = part

    @pl.when(i != 0)
    def _():
        l1_ref[...] += part


def _hyena_filter_taps(n, w1, b1, w2, b2, freq, w3):
    t = jnp.linspace(0.0, 1.0, n, dtype=F32)[:, None]
    w = (2.0 * math.pi / n) * jnp.arange(n, dtype=F32)[:, None]
    bands = jnp.linspace(1e-4, HY_BANDS - 1, HY_BANDS, dtype=F32)
    z = jnp.concatenate([t, jnp.cos(bands * w), -jnp.sin(bands * w)], axis=-1)
    emb = 48
    z = jnp.pad(z, ((0, 0), (0, emb - HY_EMB)))
    w1p = jnp.pad(w1, ((0, emb - HY_EMB), (0, 0)))
    max_decay = math.log(HY_TARGET) / HY_FAST_DECAY
    min_decay = math.log(HY_TARGET) / HY_SLOW_DECAY
    deltas = jnp.linspace(min_decay, max_decay, HY_ORDER * HY_WIDTH, dtype=F32)
    decay = jnp.exp(-t * jnp.abs(deltas)[None, :])
    tr = min(n, 512)
    hid = w2.shape[0]
    half = HY_ORDER * HY_WIDTH
    const = lambda i: (0, 0)
    return pl.pallas_call(
        _filter_kernel,
        grid=(n // tr,),
        in_specs=[
            pl.BlockSpec((tr, emb), lambda i: (i, 0)),
            pl.BlockSpec((tr, half), lambda i: (i, 0)),
            pl.BlockSpec((emb, hid), const),
            pl.BlockSpec((1, hid), const),
            pl.BlockSpec((hid, hid), const),
            pl.BlockSpec((1, hid), const),
            pl.BlockSpec((1, hid), const),
            pl.BlockSpec((hid, 2 * half), const),
        ],
        out_specs=[
            pl.BlockSpec((2 * HY_ORDER, tr, HY_WIDTH), lambda i: (0, i, 0)),
            pl.BlockSpec((1, half), const),
        ],
        out_shape=[
            jax.ShapeDtypeStruct((2 * HY_ORDER, n, HY_WIDTH), F32),
            jax.ShapeDtypeStruct((1, half), F32),
        ],
        compiler_params=_cparams("arbitrary"),
        name="hyena_filter_taps",
    )(z, decay, w1p, b1.reshape(1, -1), w2, b2.reshape(1, -1), freq.reshape(1, -1), w3)


@functools.lru_cache(maxsize=None)
def _dft_tables(n1):
    n2 = FFT_N2
    n = n1 * n2
    half = n1 // 2
    k1 = np.arange(half + 1)[:, None].astype(np.float64)
    t1 = np.arange(half)[None, :].astype(np.float64)
    ang_a = 2.0 * np.pi * k1 * t1 / n1
    eye = np.eye(SUBLANES)
    wa_fwd = np.kron(np.concatenate([np.cos(ang_a), -np.sin(ang_a)], axis=0), eye)
    weight = np.where(k1[:half] == 0, 1.0, 2.0)
    wa_inv = np.kron(np.concatenate([(weight * np.cos(ang_a[:half])).T,
                                     (-weight * np.sin(ang_a[:half])).T], axis=1), eye)
    kk1 = np.arange(half + 1)[:, None, None].astype(np.float64)
    kk2 = np.arange(n2)[None, :, None].astype(np.float64)
    tt2 = np.arange(n2)[None, None, :].astype(np.float64)
    ang_b = 2.0 * np.pi * tt2 * (kk1 + n1 * kk2) / n
    er, ei = np.cos(ang_b), -np.sin(ang_b)
    mf = np.concatenate([np.concatenate([er, -ei], axis=2),
                         np.concatenate([ei, er], axis=2)], axis=1)
    mi = np.swapaxes(mf, 1, 2)
    f32 = lambda a: np.ascontiguousarray(a.astype(np.float32))
    return f32(wa_fwd), f32(wa_inv), f32(mf), f32(mi)


def _tile_rows(ref, g, count):
    start = pl.multiple_of(g * SUBLANES, SUBLANES)
    return jnp.concatenate([ref[pl.ds(t * FFT_N2 + start, SUBLANES), :] for t in range(count)], axis=0)


def _dft_forward(u_ref, a_ref, wa_ref, mf_ref, half, emit):
    ks = half + 1

    def stage_a(g, carry):
        start = pl.multiple_of(g * SUBLANES, SUBLANES)
        r = _dot(wa_ref[...], _tile_rows(u_ref, g, half).astype(BF16))
        for part in range(2):
            for k1 in range(ks):
                src = (part * ks + k1) * SUBLANES
                a_ref[part, pl.ds(k1 * FFT_N2 + start, SUBLANES), :] = r[src:src + SUBLANES]
        return carry

    lax.fori_loop(0, FFT_N2 // SUBLANES, stage_a, 0, unroll=2)

    group = 3 if ks % 3 == 0 else 1

    def stage_b(i, carry):
        base = pl.multiple_of(i * (group * FFT_N2), FFT_N2)
        rows = [pl.ds(base + j * FFT_N2, FFT_N2) for j in range(group)]
        xs = [_dot(mf_ref[i * group + j],
                   jnp.concatenate([a_ref[0, rows[j], :], a_ref[1, rows[j], :]], axis=0).astype(BF16))
              for j in range(group)]
        emit(i * group, rows, xs)
        return carry

    lax.fori_loop(0, ks // group, stage_b, 0)


def _spectrum_kernel(t_ref, wa_ref, mf_ref, l1_ref, h_ref, a_ref, *, half, scale):
    d = pl.program_id(1)
    inv = scale / l1_ref[0]

    def emit(k0, rows, xs):
        @pl.when(d == 0)
        def _():
            for j, x in enumerate(xs):
                h_ref[0, 0, k0 + j] = x[:FFT_N2] * inv
                h_ref[0, 1, k0 + j] = x[FFT_N2:] * inv

        @pl.when(d != 0)
        def _():
            for j, x in enumerate(xs):
                h_ref[0, 0, k0 + j] += x[:FFT_N2] * inv
                h_ref[0, 1, k0 + j] -= x[FFT_N2:] * inv

    _dft_forward(t_ref, a_ref, wa_ref, mf_ref, half, emit)


def _filter_spectrum(taps, wa_fwd, mf, l1, n1):
    _, n, c = taps.shape
    half = n1 // 2
    ks = half + 1
    return pl.pallas_call(
        functools.partial(_spectrum_kernel, half=half, scale=1.0 / (n1 * FFT_N2)),
        grid=(HY_ORDER, 2),
        in_specs=[pl.BlockSpec((None, n, c), lambda o, d: (d * HY_ORDER + o, 0, 0)),
                  pl.BlockSpec(wa_fwd.shape, lambda o, d: (0, 0)),
                  _resident(mf.shape, lambda o, d: (0, 0, 0)),
                  pl.BlockSpec((1, 1, c), lambda o, d: (o, 0, 0))],
        out_specs=pl.BlockSpec((1, 2, ks, FFT_N2, c), lambda o, d: (o, 0, 0, 0, 0)),
        out_shape=jax.ShapeDtypeStruct((HY_ORDER, 2, ks, FFT_N2, c), F32),
        scratch_shapes=[pltpu.VMEM((2, ks * FFT_N2, c), F32)],
        compiler_params=_cparams("parallel", "arbitrary"),
        name="hyena_filter_spectrum",
    )(taps, wa_fwd, mf, l1.reshape(HY_ORDER, 1, c))


def _long_conv_kernel(u_ref, g_ref, skip_ref, wa_ref, wi_ref, mf_ref, mi_ref, h_ref, o_ref, a_ref, *, half):
    u_ref, g_ref, o_ref = u_ref.at[0], g_ref.at[0], o_ref.at[0]

    def emit(k0, rows, xs):
        cs = []
        for j, x in enumerate(xs):
            xr, xi = x[:FFT_N2], x[FFT_N2:]
            hr, hi = h_ref[0, k0 + j], h_ref[1, k0 + j]
            y = jnp.concatenate([xr * hr - xi * hi, xr * hi + xi * hr], axis=0).astype(BF16)
            cs.append(_dot(mi_ref[k0 + j], y))
        for j, c in enumerate(cs):
            a_ref[0, rows[j], :] = c[:FFT_N2]
            a_ref[1, rows[j], :] = c[FFT_N2:]

    _dft_forward(u_ref, a_ref, wa_ref, mf_ref, half, emit)

    tile_rows = half * SUBLANES
    alt = (1 - 2 * ((lax.broadcasted_iota(jnp.int32, (tile_rows, 1), 0) // SUBLANES) & 1)).astype(F32)
    skip = skip_ref[...]

    def stage_a_inv(g, carry):
        start = pl.multiple_of(g * SUBLANES, SUBLANES)
        c = jnp.concatenate([_tile_rows(a_ref.at[0], g, half), _tile_rows(a_ref.at[1], g, half)], axis=0)
        nyq = a_ref[0, pl.ds(half * FFT_N2 + start, SUBLANES), :]
        y = _dot(wi_ref[...], c.astype(BF16)) + alt * jnp.concatenate([nyq] * half, axis=0)
        out = _tile_rows(g_ref, g, half) * (y + _tile_rows(u_ref, g, half) * skip)
        for t1 in range(half):
            o_ref[pl.ds(t1 * FFT_N2 + start, SUBLANES), :] = out[t1 * SUBLANES:(t1 + 1) * SUBLANES]
        return carry

    lax.fori_loop(0, FFT_N2 // SUBLANES, stage_a_inv, 0, unroll=2)


def _long_conv(u, gate, skip_row, tables, hspec, order, n1):
    b, n, c = u.shape
    half = n1 // 2
    ks = half + 1
    wa_fwd, wa_inv, mf, mi = tables
    seq = pl.BlockSpec((1, n, c), lambda i: (i, 0, 0))
    const2 = lambda i: (0, 0)
    const3 = lambda i: (0, 0, 0)
    return pl.pallas_call(
        functools.partial(_long_conv_kernel, half=half),
        grid=(b,),
        in_specs=[seq, seq,
                  pl.BlockSpec((1, c), const2),
                  pl.BlockSpec(wa_fwd.shape, const2),
                  pl.BlockSpec(wa_inv.shape, const2),
                  _resident(mf.shape, const3),
                  _resident(mi.shape, const3),
                  _resident((None, 2, ks, FFT_N2, c), lambda i: (order, 0, 0, 0, 0))],
        out_specs=seq,
        out_shape=jax.ShapeDtypeStruct(u.shape, F32),
        scratch_shapes=[pltpu.VMEM((2, ks * FFT_N2, c), F32)],
        compiler_params=_cparams("parallel", vmem=LONG_CONV_VMEM_BYTES),
        name="hyena_long_conv",
    )(u, gate, skip_row, wa_fwd, wa_inv, mf, mi, hspec)


def _hyena(p, conv_w, conv_b, filt_params, skip, tables):
    b, n, _ = p.shape
    n1 = 2 * n // FFT_N2
    wa_fwd, _, mf, _ = tables
    v, x1, x2 = _short_conv(p, conv_w, conv_b)
    taps, l1 = _hyena_filter_taps(n, *filt_params)
    hspec = _filter_spectrum(taps, wa_fwd, mf, l1, n1)
    z = _long_conv(v, x1, skip[0:1], tables, hspec, 0, n1)
    return _long_conv(z, x2, skip[1:2], tables, hspec, 1, n1)


def _merge_ffn_kernel(x_ref, mod_ref, g_ref, ya_ref, yb_ref, yc_ref, gw_ref, gb_ref, wbr_ref, wo_ref,
                      wu_ref, wd_ref, o_ref):
    x = x_ref[0]
    d = x.shape[1]
    h = _adaln(x, g_ref[1:2, :], mod_ref[0, 3:4, :], mod_ref[0, 4:5, :]).astype(BF16)
    branches = ((ya_ref[0], 0, DA_WIDTH),
                (yb_ref[0].astype(BF16), DA_WIDTH, DA_WIDTH + HY_WIDTH),
                (yc_ref[0], DA_WIDTH + HY_WIDTH, DA_WIDTH + HY_WIDTH + SG_WIDTH))
    m = None
    for i, (y, lo, hi) in enumerate(branches):
        gate = _sigmoid(_dot(h, gw_ref[:, i * d:(i + 1) * d]) + gb_ref[:, i * d:(i + 1) * d])
        part = gate * _dot(y, wbr_ref[lo:hi, :])
        m = part if m is None else m + part
    x = x + mod_ref[0, 5:6, :] * _dot(m.astype(BF16), wo_ref[...])
    o_ref[0] = _ffn_apply(x, mod_ref, 2, g_ref[2:3, :], wu_ref, wd_ref)


def _merge_ffn(x, mod, mod_row, layer, norm_g, ya, yb, yc, gate_w, gate_b, w_br, w_o, ffn_up, ffn_down):
    b, s, d = x.shape
    tm = min(s, 512)
    tok = lambda i, j: (i, j, 0)
    return pl.pallas_call(
        _merge_ffn_kernel,
        grid=(b, s // tm),
        in_specs=[
            pl.BlockSpec((1, tm, d), tok),
            _mod_spec(layer, mod_row),
            pl.BlockSpec((None, 3, d), lambda i, j: (layer, 0, 0)),
            pl.BlockSpec((1, tm, DA_WIDTH), tok),
            pl.BlockSpec((1, tm, HY_WIDTH), tok),
            pl.BlockSpec((1, tm, SG_WIDTH), tok),
            _layer_spec(gate_w, layer),
            pl.BlockSpec((None, 1, gate_w.shape[2]), lambda i, j: (layer, 0, 0)),
            _layer_spec(w_br, layer),
            _layer_spec(w_o, layer),
            _layer_spec(ffn_up, layer, 1),
            _layer_spec(ffn_down, layer, 1),
        ],
        out_specs=pl.BlockSpec((1, tm, d), tok),
        out_shape=jax.ShapeDtypeStruct((b, s, d), F32),
        compiler_params=_cparams("parallel", "parallel"),
        name="merge_ffn",
    )(x, mod, norm_g, ya, yb, yc, gate_w, gate_b.reshape(gate_b.shape[0], 1, -1), w_br, w_o, ffn_up, ffn_down)


def _rope_tables(n_tokens):
    t = np.arange(n_tokens)
    pos = np.stack([t // GRID_W, t % GRID_W], axis=-1).astype(np.float64)
    inv = ROPE_BASE ** (-np.arange(ROPE_PAIRS, dtype=np.float64) / ROPE_PAIRS)
    ang = pos[:, :, None] * inv
    cos, sin = np.cos(ang), np.sin(ang)
    cos64 = np.concatenate([cos[:, 0], cos[:, 0], cos[:, 1], cos[:, 1]], axis=-1)
    sin64 = np.concatenate([-sin[:, 0], sin[:, 0], -sin[:, 1], sin[:, 1]], axis=-1)
    as_f32 = lambda a: np.tile(a, (1, LANES // DA_DH)).astype(np.float32)
    return as_f32(cos64), as_f32(sin64)


def _device_tables(n1):
    return tuple(jnp.asarray(t).astype(BF16) for t in _dft_tables(n1))


def kernel(x, c, ctx, c_ctx, ada_w, ada_b, norm_g, ffn_up, ffn_down, w_in, da_qk_gain, da_lambda, da_subln, hy_conv_w, hy_conv_b, hy_f_w1, hy_f_b1, hy_f_w2, hy_f_b2, hy_f_freq, hy_f_w3, hy_skip, sg_ln_g, sg_ln_b, sg_w, sg_b, gate_w, gate_b, w_br, w_o):
    batch, seq, d = x.shape
    cos_t, sin_t = _rope_tables(seq)
    ones_bd = np.kron(np.eye(256 // DA_DH), np.ones((DA_DH, DA_DH))).astype(BF16)
    tables_l = _device_tables(2 * seq // FFT_N2)
    tables_c = _device_tables(2 * ctx.shape[1] // FFT_N2)

    cc = jnp.concatenate([c, c_ctx[None], jnp.zeros((MOD_ROWS - batch - 1, d), F32)], axis=0)
    mod = _modulation(cc, ada_w, ada_b).reshape(DEPTH, MOD_ROWS, N_MOD, d)

    ffn_up_b = ffn_up.astype(BF16)
    ffn_down_b = ffn_down.astype(BF16)
    w_in_b = w_in.astype(BF16)
    gate_w_b = gate_w.astype(BF16)
    w_br_b = w_br.astype(BF16)
    w_o_b = w_o.astype(BF16)
    sg_w_b = sg_w.astype(BF16)

    xl, xc = x, ctx
    ctx_row = batch
    for l in range(DEPTH):
        last = l == DEPTH - 1
        filt_params = (hy_f_w1[l], hy_f_b1[l], hy_f_w2[l], hy_f_b2[l], hy_f_freq[l], hy_f_w3[l])
        qk_gain_row = jnp.concatenate([jnp.tile(da_qk_gain[l, 0], 2 * DA_HEADS),
                                       jnp.tile(da_qk_gain[l, 1], 2 * DA_HEADS)]).reshape(1, -1)
        ws_stack = sg_w_b[l].reshape(SG_GROUPS * SG_CHUNK, SG_CHUNK)
        bs_tile = jnp.repeat(sg_b[l].T, SG_WIDTH // SG_GROUPS, axis=1)

        head = functools.partial(_ffn_mixer, mod=mod, layer=l, norm_g=norm_g, ffn_up=ffn_up_b, ffn_down=ffn_down_b,
                                 w_in=w_in_b, qk_gain_row=qk_gain_row, cos_t=cos_t, sin_t=sin_t, ones_bd=ones_bd,
                                 ln_g=sg_ln_g[l], ln_b=sg_ln_b[l], ws_stack=ws_stack, bs_tile=bs_tile)
        xl, ql, kl, vl, hy_l, yc_l = head(xl, mod_row=None, rope=True)
        xc, qc, kc, vc, hy_c, yc_c = head(xc, mod_row=ctx_row, rope=False)

        ya_l = _attention(ql, (kl, kc), (vl, vc), da_qk_gain[l], da_lambda[l], da_subln[l], l)
        yb_l = _hyena(hy_l, hy_conv_w[l], hy_conv_b[l], filt_params, hy_skip[l], tables_l)
        tail = functools.partial(_merge_ffn, mod=mod, layer=l, norm_g=norm_g, gate_w=gate_w_b, gate_b=gate_b,
                                 w_br=w_br_b, w_o=w_o_b, ffn_up=ffn_up_b, ffn_down=ffn_down_b)
        xl = tail(xl, mod_row=None, ya=ya_l, yb=yb_l, yc=yc_l)

        if not last:
            ya_c = _attention(qc, (kc,), (vc,), da_qk_gain[l], da_lambda[l], da_subln[l], l)
            yb_c = _hyena(hy_c, hy_conv_w[l], hy_conv_b[l], filt_params, hy_skip[l], tables_c)
            xc = tail(xc, mod_row=ctx_row, ya=ya_c, yb=yb_c, yc=yc_c)
    return xl
```

```python
import functools
import math

import numpy as np
import jax
import jax.numpy as jnp
from jax import lax
from jax.experimental import pallas as pl
from jax.experimental.pallas import tpu as pltpu

F32 = jnp.float32
BF16 = jnp.bfloat16

D_MODEL = 1024
DEPTH = 2
GRID_W = 64
EPS = 1e-6
N_MOD = 9
FFN_HIDDEN = 2816
FFN_CHUNKS = ((0, 1536), (1536, FFN_HIDDEN))

DA_HEADS = 4
DA_DH = 64
DA_DV = 2 * DA_DH
DA_WIDTH = DA_HEADS * DA_DV
ROPE_BASE = 10000.0
ROPE_PAIRS = DA_DH // 4

HY_WIDTH = 256
HY_ORDER = 2
HY_EMB = 33
HY_BANDS = (HY_EMB - 1) // 2
HY_FAST_DECAY = 0.3
HY_SLOW_DECAY = 1.5
HY_TARGET = 1e-2

SG_WIDTH = 256
SG_GROUPS = 4
SG_CHUNK = 128

HY_OFF = 3 * DA_WIDTH
SG_OFF = HY_OFF + (HY_ORDER + 1) * HY_WIDTH
IN_WIDTH = SG_OFF + 2 * SG_WIDTH

VMEM_LIMIT_BYTES = 56 * 1024 * 1024
LONG_CONV_VMEM_BYTES = 60 * 1024 * 1024
LANES = 128
SUBLANES = 8
FFT_N2 = 128
MOD_ROWS = 16
Q_SCALE = math.log2(math.e) * DA_DH ** -0.5
ATTN_Q_TILE = 512
ATTN_KEY_BLOCK = 256
ATTN_VT_ROWS = DA_DV + 16
ATTN_BOUND_MARGIN = 1.02
ATTN_MIN_MASS = 2.0 ** -60
ATTN_M_INIT = -1e30


def _cparams(*sem, vmem=VMEM_LIMIT_BYTES):
    return pltpu.CompilerParams(dimension_semantics=sem, vmem_limit_bytes=vmem)


def _resident(shape, index_map):
    return pl.BlockSpec(shape, index_map, pipeline_mode=pl.Buffered(1))


def _dot(a, b):
    return jnp.dot(a, b, preferred_element_type=F32)


def _split(a):
    hi = a.astype(BF16)
    lo = (a - hi.astype(F32)).astype(BF16)
    return hi, lo


def _dot3(a, b):
    ah, al = _split(a)
    bh, bl = _split(b)
    return _dot(ah, bh) + (_dot(al, bh) + _dot(ah, bl))


def _sigmoid(x):
    return 1.0 / (1.0 + jnp.exp(-x))


def _adaln(x, gain, shift, scale):
    y = x * lax.rsqrt(jnp.mean(x * x, axis=-1, keepdims=True) + EPS)
    return (y * gain) * (1.0 + scale) + shift


def _mod_kernel(c_ref, w_ref, b_ref, o_ref):
    c = c_ref[...]
    s = c * _sigmoid(c)
    o_ref[0] = _dot(s.astype(BF16), w_ref[0].astype(BF16)) + b_ref[0]


def _modulation(cc, ada_w, ada_b):
    depth, d, n = ada_w.shape
    tn = 1536
    return pl.pallas_call(
        _mod_kernel,
        grid=(depth, n // tn),
        in_specs=[
            pl.BlockSpec((MOD_ROWS, d), lambda l, j: (0, 0)),
            pl.BlockSpec((1, d, tn), lambda l, j: (l, 0, j)),
            pl.BlockSpec((1, 1, tn), lambda l, j: (l, 0, j)),
        ],
        out_specs=pl.BlockSpec((1, MOD_ROWS, tn), lambda l, j: (l, 0, j)),
        out_shape=jax.ShapeDtypeStruct((depth, MOD_ROWS, n), F32),
        compiler_params=_cparams("parallel", "parallel"),
        name="modulation",
    )(cc, ada_w, ada_b.reshape(depth, 1, n))


def _ffn_apply(x, mod_ref, sub, gain, wu_ref, wd_ref):
    shift = mod_ref[0, 3 * sub:3 * sub + 1, :]
    scale = mod_ref[0, 3 * sub + 1:3 * sub + 2, :]
    gate = mod_ref[0, 3 * sub + 2:3 * sub + 3, :]
    h = _adaln(x, gain, shift, scale).astype(BF16)
    acc = None
    for lo, hi in FFN_CHUNKS:
        a = _dot(h, wu_ref[:, lo:hi])
        b = _dot(h, wu_ref[:, FFN_HIDDEN + lo:FFN_HIDDEN + hi])
        part = _dot((a * _sigmoid(a) * b).astype(BF16), wd_ref[lo:hi, :])
        acc = part if acc is None else acc + part
    return x + (0.5 * gate) * acc


def _group_sums(sq, ones_bd):
    outs = []
    for j in range(sq.shape[1] // 256):
        hi, lo = _split(sq[:, 256 * j:256 * (j + 1)])
        outs.append(_dot(hi, ones_bd) + _dot(lo, ones_bd))
    return jnp.concatenate(outs, axis=-1)


def _swap16(x):
    n = x.shape[-1]
    lane = lax.broadcasted_iota(jnp.int32, x.shape, x.ndim - 1)
    up = pltpu.roll(x, 16, x.ndim - 1)
    dn = pltpu.roll(x, n - 16, x.ndim - 1)
    return jnp.where((lane & 16) != 0, up, dn)


def _ffn_mixer_kernel(x_ref, mod_ref, g_ref, wu_ref, wd_ref, w_ref, qkg_ref, cos_ref, sin_ref, ones_ref,
                      lng_ref, lnb_ref, ws_ref, bs_ref,
                      x1_ref, q_ref, k_ref, vt_ref, hy_ref, yc_ref, *, rope):
    x = _ffn_apply(x_ref[0], mod_ref, 0, g_ref[0:1, :], wu_ref, wd_ref)
    x1_ref[0] = x
    tm = x.shape[0]
    h = _adaln(x, g_ref[1:2, :], mod_ref[0, 3:4, :], mod_ref[0, 4:5, :]).astype(BF16)

    qk = _dot(h, w_ref[:, :2 * DA_WIDTH])
    v = _dot(h, w_ref[:, 2 * DA_WIDTH:HY_OFF])
    hy_ref[0] = _dot(h, w_ref[:, HY_OFF:SG_OFF])
    sg = _dot(h, w_ref[:, SG_OFF:])

    ss = _group_sums(qk * qk, ones_ref[...])
    qk = qk * lax.rsqrt(ss * (1.0 / DA_DH) + EPS) * qkg_ref[...]
    if rope:
        reps = 2 * DA_WIDTH // LANES
        cosf = jnp.concatenate([cos_ref[...]] * reps, axis=-1)
        sinf = jnp.concatenate([sin_ref[...]] * reps, axis=-1)
        qk = qk * cosf + _swap16(qk) * sinf
    q_ref[0] = (qk[:, :DA_WIDTH] * Q_SCALE).astype(BF16)
    k_ref[0] = qk[:, DA_WIDTH:].astype(BF16)

    fill = (lax.broadcasted_iota(jnp.int32, (ATTN_VT_ROWS - DA_DV, tm), 0) == 0).astype(BF16)
    for hd in range(DA_HEADS):
        vt_ref[0, hd * ATTN_VT_ROWS:hd * ATTN_VT_ROWS + DA_DV, :] = v[:, hd * DA_DV:(hd + 1) * DA_DV].T.astype(BF16)
        vt_ref[0, hd * ATTN_VT_ROWS + DA_DV:(hd + 1) * ATTN_VT_ROWS, :] = fill

    sg = 0.5 * sg * (1.0 + lax.erf(sg * (2.0 ** -0.5)))
    u = sg[:, :SG_WIDTH]
    vv = sg[:, SG_WIDTH:]
    mu = jnp.mean(vv, axis=-1, keepdims=True)
    var = jnp.mean(jnp.square(vv - mu), axis=-1, keepdims=True)
    vv = ((vv - mu) * lax.rsqrt(var + EPS) * lng_ref[...] + lnb_ref[...]).astype(BF16)
    group = lax.broadcasted_iota(jnp.int32, (SG_CHUNK, SG_WIDTH), 1) // (SG_WIDTH // SG_GROUPS)
    for c in range(tm // SG_CHUNK):
        rows = slice(c * SG_CHUNK, (c + 1) * SG_CHUNK)
        mixed = _dot(ws_ref[...], vv[rows])
        sel = mixed[:SG_CHUNK]
        for g in range(1, SG_GROUPS):
            sel = jnp.where(group == g, mixed[g * SG_CHUNK:(g + 1) * SG_CHUNK], sel)
        yc_ref[0, rows, :] = (u[rows] * (sel + bs_ref[...])).astype(BF16)


def _layer_spec(arr, layer, *lead):
    idx = (layer,) + lead
    return _resident((None,) * len(idx) + arr.shape[len(idx):], lambda i, j: idx + (0, 0))


def _mod_spec(layer, row):
    d = D_MODEL
    if row is None:
        return pl.BlockSpec((None, 1, N_MOD, d), lambda i, j: (layer, i, 0, 0))
    return pl.BlockSpec((None, 1, N_MOD, d), lambda i, j: (layer, row, 0, 0))


def _ffn_mixer(x, mod, mod_row, layer, norm_g, ffn_up, ffn_down, w_in, qk_gain_row, cos_t, sin_t, ones_bd,
               ln_g, ln_b, ws_stack, bs_tile, rope):
    b, s, d = x.shape
    tm = min(s, 512)
    const = lambda i, j: (0, 0)
    tok = lambda i, j: (i, j, 0)
    hy_w = SG_OFF - HY_OFF
    return pl.pallas_call(
        functools.partial(_ffn_mixer_kernel, rope=rope),
        grid=(b, s // tm),
        in_specs=[
            pl.BlockSpec((1, tm, d), tok),
            _mod_spec(layer, mod_row),
            pl.BlockSpec((None, 3, d), lambda i, j: (layer, 0, 0)),
            _layer_spec(ffn_up, layer, 0),
            _layer_spec(ffn_down, layer, 0),
            _layer_spec(w_in, layer),
            pl.BlockSpec((1, 2 * DA_WIDTH), const),
            pl.BlockSpec((tm, LANES), lambda i, j: (j, 0)),
            pl.BlockSpec((tm, LANES), lambda i, j: (j, 0)),
            pl.BlockSpec((256, 256), const),
            pl.BlockSpec((1, SG_WIDTH), const),
            pl.BlockSpec((1, SG_WIDTH), const),
            pl.BlockSpec((SG_GROUPS * SG_CHUNK, SG_CHUNK), const),
            pl.BlockSpec((SG_CHUNK, SG_WIDTH), const),
        ],
        out_specs=[
            pl.BlockSpec((1, tm, d), tok),
            pl.BlockSpec((1, tm, DA_WIDTH), tok),
            pl.BlockSpec((1, tm, DA_WIDTH), tok),
            pl.BlockSpec((1, DA_HEADS * ATTN_VT_ROWS, tm), lambda i, j: (i, 0, j)),
            pl.BlockSpec((1, tm, hy_w), tok),
            pl.BlockSpec((1, tm, SG_WIDTH), tok),
        ],
        out_shape=[
            jax.ShapeDtypeStruct((b, s, d), F32),
            jax.ShapeDtypeStruct((b, s, DA_WIDTH), BF16),
            jax.ShapeDtypeStruct((b, s, DA_WIDTH), BF16),
            jax.ShapeDtypeStruct((b, DA_HEADS * ATTN_VT_ROWS, s), BF16),
            jax.ShapeDtypeStruct((b, s, hy_w), F32),
            jax.ShapeDtypeStruct((b, s, SG_WIDTH), BF16),
        ],
        compiler_params=_cparams("parallel", "parallel"),
        name="ffn_mixer_in_rope" if rope else "ffn_mixer_in",
    )(x, mod, norm_g, ffn_up, ffn_down, w_in, qk_gain_row, cos_t, sin_t, ones_bd,
      ln_g.reshape(1, -1), ln_b.reshape(1, -1), ws_stack, bs_tile)


def _attn_kernel(*refs, lam_init, n_parts):
    bound_ref, q_ref = refs[:2]
    k_refs = refs[2:2 + n_parts]
    vt_refs = refs[2 + n_parts:2 + 2 * n_parts]
    lv_ref, sub_ref, o_ref, p_ref, acc_ref, o_scr = refs[2 + 2 * n_parts:]
    lv = lv_ref[...]
    lam = (jnp.exp(jnp.sum(lv[0:1] * lv[1:2], axis=-1, keepdims=True))
           - jnp.exp(jnp.sum(lv[2:3] * lv[3:4], axis=-1, keepdims=True)) + lam_init)
    q = q_ref[0]
    tq = q.shape[0]
    nt = (((1,), (1,)), ((), ()))
    bound = bound_ref[...]
    sizes = [k_ref.shape[1] for k_ref in k_refs]
    bases = [sum(sizes[:i]) for i in range(n_parts)]

    def block_scores(k_ref, start, c):
        cols = slice(c * DA_DH, (c + 1) * DA_DH)
        kblk = k_ref[0, pl.ds(start, ATTN_KEY_BLOCK), cols]
        return lax.dot_general(kblk, q[:, cols], nt, preferred_element_type=F32)

    mass = [None, None]
    for k_ref, base, size in zip(k_refs, bases, sizes):
        for start in range(0, size, ATTN_KEY_BLOCK):
            for c in range(2):
                p = jnp.exp2(block_scores(k_ref, start, c) - bound)
                part = jnp.sum(p.reshape(ATTN_KEY_BLOCK // SUBLANES, SUBLANES, tq), axis=0)
                mass[c] = part if mass[c] is None else mass[c] + part
                p_ref[c, base + start:base + start + ATTN_KEY_BLOCK, :] = p.astype(BF16)
    l0, l1 = (jnp.sum(m, axis=0, keepdims=True) for m in mass)
    ratio = (lam * l0 / l1).astype(BF16)
    o = None
    for vt_ref, base, size in zip(vt_refs, bases, sizes):
        for start in range(base, base + size, ATTN_KEY_BLOCK):
            rows = slice(start, start + ATTN_KEY_BLOCK)
            p_ref[0, rows, :] = p_ref[0, rows, :] - ratio * p_ref[1, rows, :]
        part = _dot(vt_ref[0, :DA_DV, :], p_ref[0, base:base + size, :])
        o = part if o is None else o + part
    o_scr[...] = o * (1.0 / l0)

    @pl.when(jnp.min(jnp.minimum(l0, l1)) < ATTN_MIN_MASS)
    def _():
        acc_ref[...] = jnp.zeros(acc_ref.shape, F32)
        maxima = (jnp.full((1, tq), ATTN_M_INIT, F32),) * 2
        for k_ref, vt_ref, size in zip(k_refs, vt_refs, sizes):
            def slow_block(j, maxima, k_ref=k_ref, vt_ref=vt_ref):
                start = pl.multiple_of(j * ATTN_KEY_BLOCK, ATTN_KEY_BLOCK)
                out = []
                for c in range(2):
                    s = block_scores(k_ref, start, c)
                    m_new = jnp.maximum(maxima[c], jnp.max(s, axis=0, keepdims=True))
                    vblk = vt_ref[0, :, pl.ds(start, ATTN_KEY_BLOCK)]
                    acc_ref[c] = (jnp.exp2(maxima[c] - m_new) * acc_ref[c]
                                  + _dot(vblk, jnp.exp2(s - m_new).astype(BF16)))
                    out.append(m_new)
                return tuple(out)

            maxima = lax.fori_loop(0, size // ATTN_KEY_BLOCK, slow_block, maxima)
        o_scr[...] = (acc_ref[0, :DA_DV, :] * (1.0 / acc_ref[0, DA_DV:DA_DV + 1, :])
                      - acc_ref[1, :DA_DV, :] * (lam / acc_ref[1, DA_DV:DA_DV + 1, :]))

    o = o_scr[...]
    y = o * lax.rsqrt(jnp.mean(o * o, axis=0, keepdims=True) + EPS)
    sub = jnp.concatenate([sub_ref[...]] * (tq // LANES), axis=-1)
    o_ref[0] = (y * sub * (1.0 - lam_init)).T.astype(BF16)


def _attention(q, k_parts, vt_parts, qk_gain, lam_vecs, subln, layer_idx):
    b, sq, _ = q.shape
    sk = sum(k.shape[1] for k in k_parts)
    tq = min(sq, ATTN_Q_TILE)
    lam_init = 0.8 - 0.6 * math.exp(-0.3 * layer_idx)
    bound = (ATTN_BOUND_MARGIN * Q_SCALE * DA_DH) * jnp.max(jnp.abs(qk_gain[0])) * jnp.max(jnp.abs(qk_gain[1]))
    return pl.pallas_call(
        functools.partial(_attn_kernel, lam_init=lam_init, n_parts=len(k_parts)),
        grid=(b, DA_HEADS, sq // tq),
        in_specs=(
            [pl.BlockSpec((1, 1), lambda i, h, j: (0, 0)),
             pl.BlockSpec((1, tq, DA_DV), lambda i, h, j: (i, j, h))]
            + [pl.BlockSpec((1, k.shape[1], DA_DV), lambda i, h, j: (i, 0, h)) for k in k_parts]
            + [pl.BlockSpec((1, ATTN_VT_ROWS, vt.shape[2]), lambda i, h, j: (i, h, 0)) for vt in vt_parts]
            + [pl.BlockSpec((4, DA_DH), lambda i, h, j: (0, 0)),
               pl.BlockSpec((DA_DV, LANES), lambda i, h, j: (0, 0))]),
        out_specs=pl.BlockSpec((1, tq, DA_DV), lambda i, h, j: (i, j, h)),
        out_shape=jax.ShapeDtypeStruct((b, sq, DA_WIDTH), BF16),
        scratch_shapes=[pltpu.VMEM((2, sk, tq), BF16),
                        pltpu.VMEM((2, ATTN_VT_ROWS, tq), F32),
                        pltpu.VMEM((DA_DV, tq), F32)],
        compiler_params=_cparams("parallel", "parallel", "parallel"),
        name="diff_attention",
    )(bound.reshape(1, 1), q, *k_parts, *vt_parts, lam_vecs, jnp.broadcast_to(subln[:, None], (DA_DV, LANES)))


def _short_conv_kernel(p0_ref, p1_ref, p2_ref, w0_ref, w1_ref, w2_ref, b0_ref, b1_ref, b2_ref,
                       o0_ref, o1_ref, o2_ref):
    n = p0_ref.shape[1]
    row = lax.broadcasted_iota(jnp.int32, (n, LANES), 0)
    for p_ref, w_ref, b_ref, o_ref in ((p0_ref, w0_ref, b0_ref, o0_ref),
                                       (p1_ref, w1_ref, b1_ref, o1_ref),
                                       (p2_ref, w2_ref, b2_ref, o2_ref)):
        p = p_ref[0]
        prev = jnp.where(row == 0, 0.0, pltpu.roll(p, 1, 0))
        nxt = jnp.where(row == n - 1, 0.0, pltpu.roll(p, n - 1, 0))
        w = w_ref[...]
        o_ref[0] = prev * w[0:1] + p * w[1:2] + nxt * w[2:3] + b_ref[...]


def _short_conv(p, conv_w, conv_b):
    b, n, _ = p.shape
    halves = HY_WIDTH // LANES
    seq = lambda part: pl.BlockSpec((1, n, LANES), lambda i, j: (i, 0, part * halves + j))
    wsp = lambda part: pl.BlockSpec((3, LANES), lambda i, j: (0, part * halves + j))
    bsp = lambda part: pl.BlockSpec((1, LANES), lambda i, j: (0, part * halves + j))
    out = pl.BlockSpec((1, n, LANES), lambda i, j: (i, 0, j))
    shp = jax.ShapeDtypeStruct((b, n, HY_WIDTH), F32)
    cb = conv_b.reshape(1, -1)
    return pl.pallas_call(
        _short_conv_kernel,
        grid=(b, halves),
        in_specs=[seq(0), seq(1), seq(2), wsp(0), wsp(1), wsp(2), bsp(0), bsp(1), bsp(2)],
        out_specs=[out, out, out],
        out_shape=[shp, shp, shp],
        compiler_params=_cparams("parallel", "parallel"),
        name="hyena_short_conv",
    )(p, p, p, conv_w, conv_w, conv_w, cb, cb, cb)


def _filter_kernel(z_ref, dec_ref, w1_ref, b1_ref, w2_ref, b2_ref, fr_ref, w3_ref, f_ref, l1_ref):
    i = pl.program_id(0)
    fr = fr_ref[...]
    h = jnp.sin(fr * (_dot3(z_ref[...], w1_ref[...]) + b1_ref[...]))
    h = jnp.sin(fr * (_dot3(h, w2_ref[...]) + b2_ref[...]))
    h = _dot3(h, w3_ref[...])
    tr = h.shape[0]
    half = HY_ORDER * HY_WIDTH
    dec = dec_ref[...]
    fwd = h[:, :half] * dec
    first = (lax.broadcasted_iota(jnp.int32, (tr, half), 0) + i * tr) == 0
    bwd = jnp.where(first, 0.0, h[:, half:] * dec)
    for o in range(HY_ORDER):
        f_ref[o] = fwd[:, o * HY_WIDTH:(o + 1) * HY_WIDTH]
        f_ref[HY_ORDER + o] = bwd[:, o * HY_WIDTH:(o + 1) * HY_WIDTH]
    part = jnp.sum(jnp.abs(fwd) + jnp.abs(bwd), axis=0, keepdims=True)

    @pl.when(i == 0)
    def _():
        l1_ref[...] = part

    @pl.when(i != 0)
    def _():
        l1_ref[...] += part


def _hyena_filter_taps(n, w1, b1, w2, b2, freq, w3):
    t = jnp.linspace(0.0, 1.0, n, dtype=F32)[:, None]
    w = (2.0 * math.pi / n) * jnp.arange(n, dtype=F32)[:, None]
    bands = jnp.linspace(1e-4, HY_BANDS - 1, HY_BANDS, dtype=F32)
    z = jnp.concatenate([t, jnp.cos(bands * w), -jnp.sin(bands * w)], axis=-1)
    emb = 48
    z = jnp.pad(z, ((0, 0), (0, emb - HY_EMB)))
    w1p = jnp.pad(w1, ((0, emb - HY_EMB), (0, 0)))
    max_decay = math.log(HY_TARGET) / HY_FAST_DECAY
    min_decay = math.log(HY_TARGET) / HY_SLOW_DECAY
    deltas = jnp.linspace(min_decay, max_decay, HY_ORDER * HY_WIDTH, dtype=F32)
    decay = jnp.exp(-t * jnp.abs(deltas)[None, :])
    tr = min(n, 512)
    hid = w2.shape[0]
    half = HY_ORDER * HY_WIDTH
    const = lambda i: (0, 0)
    return pl.pallas_call(
        _filter_kernel,
        grid=(n // tr,),
        in_specs=[
            pl.BlockSpec((tr, emb), lambda i: (i, 0)),
            pl.BlockSpec((tr, half), lambda i: (i, 0)),
            pl.BlockSpec((emb, hid), const),
            pl.BlockSpec((1, hid), const),
            pl.BlockSpec((hid, hid), const),
            pl.BlockSpec((1, hid), const),
            pl.BlockSpec((1, hid), const),
            pl.BlockSpec((hid, 2 * half), const),
        ],
        out_specs=[
            pl.BlockSpec((2 * HY_ORDER, tr, HY_WIDTH), lambda i: (0, i, 0)),
            pl.BlockSpec((1, half), const),
        ],
        out_shape=[
            jax.ShapeDtypeStruct((2 * HY_ORDER, n, HY_WIDTH), F32),
            jax.ShapeDtypeStruct((1, half), F32),
        ],
        compiler_params=_cparams("arbitrary"),
        name="hyena_filter_taps",
    )(z, decay, w1p, b1.reshape(1, -1), w2, b2.reshape(1, -1), freq.reshape(1, -1), w3)


@functools.lru_cache(maxsize=None)
def _dft_tables(n1):
    n2 = FFT_N2
    n = n1 * n2
    half = n1 // 2
    k1 = np.arange(half + 1)[:, None].astype(np.float64)
    t1 = np.arange(half)[None, :].astype(np.float64)
    ang_a = 2.0 * np.pi * k1 * t1 / n1
    eye = np.eye(SUBLANES)
    wa_fwd = np.kron(np.concatenate([np.cos(ang_a), -np.sin(ang_a)], axis=0), eye)
    weight = np.where(k1[:half] == 0, 1.0, 2.0)
    wa_inv = np.kron(np.concatenate([(weight * np.cos(ang_a[:half])).T,
                                     (-weight * np.sin(ang_a[:half])).T], axis=1), eye)
    kk1 = np.arange(half + 1)[:, None, None].astype(np.float64)
    kk2 = np.arange(n2)[None, :, None].astype(np.float64)
    tt2 = np.arange(n2)[None, None, :].astype(np.float64)
    ang_b = 2.0 * np.pi * tt2 * (kk1 + n1 * kk2) / n
    er, ei = np.cos(ang_b), -np.sin(ang_b)
    mf = np.concatenate([np.concatenate([er, -ei], axis=2),
                         np.concatenate([ei, er], axis=2)], axis=1)
    mi = np.swapaxes(mf, 1, 2)
    f32 = lambda a: np.ascontiguousarray(a.astype(np.float32))
    return f32(wa_fwd), f32(wa_inv), f32(mf), f32(mi)


def _tile_rows(ref, g, count):
    start = g * SUBLANES
    return jnp.concatenate([ref[t * FFT_N2 + start:t * FFT_N2 + start + SUBLANES, :] for t in range(count)],
                           axis=0)


def _dft_forward(u_ref, a_ref, wa_ref, mf_ref, half, emit):
    ks = half + 1

    for g in range(FFT_N2 // SUBLANES):
        start = g * SUBLANES
        r = _dot(wa_ref[...], _tile_rows(u_ref, g, half).astype(BF16))
        for part in range(2):
            for k1 in range(ks):
                src = (part * ks + k1) * SUBLANES
                a_ref[part, k1 * FFT_N2 + start:k1 * FFT_N2 + start + SUBLANES, :] = r[src:src + SUBLANES]

    group = 3 if ks % 3 == 0 else 1

    for i in range(ks // group):
        rows = [slice((i * group + j) * FFT_N2, (i * group + j + 1) * FFT_N2) for j in range(group)]
        xs = [_dot(mf_ref[i * group + j],
                   jnp.concatenate([a_ref[0, rows[j], :], a_ref[1, rows[j], :]], axis=0).astype(BF16))
              for j in range(group)]
        emit(i * group, rows, xs)


def _spectrum_kernel(t_ref, wa_ref, mf_ref, l1_ref, h_ref, a_ref, *, half, scale):
    d = pl.program_id(1)
    inv = scale / l1_ref[0]

    def emit(k0, rows, xs):
        @pl.when(d == 0)
        def _():
            for j, x in enumerate(xs):
                h_ref[0, 0, k0 + j] = x[:FFT_N2] * inv
                h_ref[0, 1, k0 + j] = x[FFT_N2:] * inv

        @pl.when(d != 0)
        def _():
            for j, x in enumerate(xs):
                h_ref[0, 0, k0 + j] += x[:FFT_N2] * inv
                h_ref[0, 1, k0 + j] -= x[FFT_N2:] * inv

    _dft_forward(t_ref, a_ref, wa_ref, mf_ref, half, emit)


def _filter_spectrum(taps, wa_fwd, mf, l1, n1):
    _, n, c = taps.shape
    half = n1 // 2
    ks = half + 1
    return pl.pallas_call(
        functools.partial(_spectrum_kernel, half=half, scale=1.0 / (n1 * FFT_N2)),
        grid=(HY_ORDER, 2),
        in_specs=[pl.BlockSpec((None, n, c), lambda o, d: (d * HY_ORDER + o, 0, 0)),
                  pl.BlockSpec(wa_fwd.shape, lambda o, d: (0, 0)),
                  _resident(mf.shape, lambda o, d: (0, 0, 0)),
                  pl.BlockSpec((1, 1, c), lambda o, d: (o, 0, 0))],
        out_specs=pl.BlockSpec((1, 2, ks, FFT_N2, c), lambda o, d: (o, 0, 0, 0, 0)),
        out_shape=jax.ShapeDtypeStruct((HY_ORDER, 2, ks, FFT_N2, c), F32),
        scratch_shapes=[pltpu.VMEM((2, ks * FFT_N2, c), F32)],
        compiler_params=_cparams("parallel", "arbitrary"),
        name="hyena_filter_spectrum",
    )(taps, wa_fwd, mf, l1.reshape(HY_ORDER, 1, c))


def _long_conv_kernel(u_ref, g_ref, skip_ref, wa_ref, wi_ref, mf_ref, mi_ref, h_ref, o_ref, a_ref, *, half):
    u_ref, g_ref, o_ref = u_ref.at[0], g_ref.at[0], o_ref.at[0]

    def emit(k0, rows, xs):
        cs = []
        for j, x in enumerate(xs):
            xr, xi = x[:FFT_N2], x[FFT_N2:]
            hr, hi = h_ref[0, k0 + j], h_ref[1, k0 + j]
            y = jnp.concatenate([xr * hr - xi * hi, xr * hi + xi * hr], axis=0).astype(BF16)
            cs.append(_dot(mi_ref[k0 + j], y))
        for j, c in enumerate(cs):
            a_ref[0, rows[j], :] = c[:FFT_N2]
            a_ref[1, rows[j], :] = c[FFT_N2:]

    _dft_forward(u_ref, a_ref, wa_ref, mf_ref, half, emit)

    tile_rows = half * SUBLANES
    alt = (1 - 2 * ((lax.broadcasted_iota(jnp.int32, (tile_rows, 1), 0) // SUBLANES) & 1)).astype(F32)
    skip = skip_ref[...]

    for g in range(FFT_N2 // SUBLANES):
        start = g * SUBLANES
        c = jnp.concatenate([_tile_rows(a_ref.at[0], g, half), _tile_rows(a_ref.at[1], g, half)], axis=0)
        nyq = a_ref[0, half * FFT_N2 + start:half * FFT_N2 + start + SUBLANES, :]
        y = _dot(wi_ref[...], c.astype(BF16)) + alt * jnp.concatenate([nyq] * half, axis=0)
        out = _tile_rows(g_ref, g, half) * (y + _tile_rows(u_ref, g, half) * skip)
        for t1 in range(half):
            o_ref[t1 * FFT_N2 + start:t1 * FFT_N2 + start + SUBLANES, :] = out[t1 * SUBLANES:(t1 + 1) * SUBLANES]


def _long_conv(u, gate, skip_row, tables, hspec, order, n1):
    b, n, c = u.shape
    half = n1 // 2
    ks = half + 1
    wa_fwd, wa_inv, mf, mi = tables
    seq = pl.BlockSpec((1, n, c), lambda i: (i, 0, 0))
    const2 = lambda i: (0, 0)
    const3 = lambda i: (0, 0, 0)
    return pl.pallas_call(
        functools.partial(_long_conv_kernel, half=half),
        grid=(b,),
        in_specs=[seq, seq,
                  pl.BlockSpec((1, c), const2),
                  pl.BlockSpec(wa_fwd.shape, const2),
                  pl.BlockSpec(wa_inv.shape, const2),
                  _resident(mf.shape, const3),
                  _resident(mi.shape, const3),
                  _resident((None, 2, ks, FFT_N2, c), lambda i: (order, 0, 0, 0, 0))],
        out_specs=seq,
        out_shape=jax.ShapeDtypeStruct(u.shape, F32),
        scratch_shapes=[pltpu.VMEM((2, ks * FFT_N2, c), F32)],
        compiler_params=_cparams("parallel", vmem=LONG_CONV_VMEM_BYTES),
        name="hyena_long_conv",
    )(u, gate, skip_row, wa_fwd, wa_inv, mf, mi, hspec)


def _hyena(p, conv_w, conv_b, filt_params, skip, tables):
    b, n, _ = p.shape
    n1 = 2 * n // FFT_N2
    wa_fwd, _, mf, _ = tables
    v, x1, x2 = _short_conv(p, conv_w, conv_b)
    taps, l1 = _hyena_filter_taps(n, *filt_params)
    hspec = _filter_spectrum(taps, wa_fwd, mf, l1, n1)
    z = _long_conv(v, x1, skip[0:1], tables, hspec, 0, n1)
    return _long_conv(z, x2, skip[1:2], tables, hspec, 1, n1)


def _merge_ffn_kernel(x_ref, mod_ref, g_ref, ya_ref, yb_ref, yc_ref, gw_ref, gb_ref, wbr_ref, wo_ref,
                      wu_ref, wd_ref, o_ref):
    x = x_ref[0]
    d = x.shape[1]
    h = _adaln(x, g_ref[1:2, :], mod_ref[0, 3:4, :], mod_ref[0, 4:5, :]).astype(BF16)
    branches = ((ya_ref[0], 0, DA_WIDTH),
                (yb_ref[0].astype(BF16), DA_WIDTH, DA_WIDTH + HY_WIDTH),
                (yc_ref[0], DA_WIDTH + HY_WIDTH, DA_WIDTH + HY_WIDTH + SG_WIDTH))
    m = None
    for i, (y, lo, hi) in enumerate(branches):
        gate = _sigmoid(_dot(h, gw_ref[:, i * d:(i + 1) * d]) + gb_ref[:, i * d:(i + 1) * d])
        part = gate * _dot(y, wbr_ref[lo:hi, :])
        m = part if m is None else m + part
    x = x + mod_ref[0, 5:6, :] * _dot(m.astype(BF16), wo_ref[...])
    o_ref[0] = _ffn_apply(x, mod_ref, 2, g_ref[2:3, :], wu_ref, wd_ref)


def _merge_ffn(x, mod, mod_row, layer, norm_g, ya, yb, yc, gate_w, gate_b, w_br, w_o, ffn_up, ffn_down):
    b, s, d = x.shape
    tm = min(s, 512)
    tok = lambda i, j: (i, j, 0)
    return pl.pallas_call(
        _merge_ffn_kernel,
        grid=(b, s // tm),
        in_specs=[
            pl.BlockSpec((1, tm, d), tok),
            _mod_spec(layer, mod_row),
            pl.BlockSpec((None, 3, d), lambda i, j: (layer, 0, 0)),
            pl.BlockSpec((1, tm, DA_WIDTH), tok),
            pl.BlockSpec((1, tm, HY_WIDTH), tok),
            pl.BlockSpec((1, tm, SG_WIDTH), tok),
            _layer_spec(gate_w, layer),
            pl.BlockSpec((None, 1, gate_w.shape[2]), lambda i, j: (layer, 0, 0)),
            _layer_spec(w_br, layer),
            _layer_spec(w_o, layer),
            _layer_spec(ffn_up, layer, 1),
            _layer_spec(ffn_down, layer, 1),
        ],
        out_specs=pl.BlockSpec((1, tm, d), tok),
        out_shape=jax.ShapeDtypeStruct((b, s, d), F32),
        compiler_params=_cparams("parallel", "parallel"),
        name="merge_ffn",
    )(x, mod, norm_g, ya, yb, yc, gate_w, gate_b.reshape(gate_b.shape[0], 1, -1), w_br, w_o, ffn_up, ffn_down)


def _rope_tables(n_tokens):
    t = np.arange(n_tokens)
    pos = np.stack([t // GRID_W, t % GRID_W], axis=-1).astype(np.float64)
    inv = ROPE_BASE ** (-np.arange(ROPE_PAIRS, dtype=np.float64) / ROPE_PAIRS)
    ang = pos[:, :, None] * inv
    cos, sin = np.cos(ang), np.sin(ang)
    cos64 = np.concatenate([cos[:, 0], cos[:, 0], cos[:, 1], cos[:, 1]], axis=-1)
    sin64 = np.concatenate([-sin[:, 0], sin[:, 0], -sin[:, 1], sin[:, 1]], axis=-1)
    as_f32 = lambda a: np.tile(a, (1, LANES // DA_DH)).astype(np.float32)
    return as_f32(cos64), as_f32(sin64)


def _device_tables(n1):
    return tuple(jnp.asarray(t).astype(BF16) for t in _dft_tables(n1))


def kernel(x, c, ctx, c_ctx, ada_w, ada_b, norm_g, ffn_up, ffn_down, w_in, da_qk_gain, da_lambda, da_subln, hy_conv_w, hy_conv_b, hy_f_w1, hy_f_b1, hy_f_w2, hy_f_b2, hy_f_freq, hy_f_w3, hy_skip, sg_ln_g, sg_ln_b, sg_w, sg_b, gate_w, gate_b, w_br, w_o):
    batch, seq, d = x.shape
    cos_t, sin_t = _rope_tables(seq)
    ones_bd = np.kron(np.eye(256 // DA_DH), np.ones((DA_DH, DA_DH))).astype(BF16)
    tables_l = _device_tables(2 * seq // FFT_N2)
    tables_c = _device_tables(2 * ctx.shape[1] // FFT_N2)

    cc = jnp.concatenate([c, c_ctx[None], jnp.zeros((MOD_ROWS - batch - 1, d), F32)], axis=0)
    mod = _modulation(cc, ada_w, ada_b).reshape(DEPTH, MOD_ROWS, N_MOD, d)

    ffn_up_b = ffn_up.astype(BF16)
    ffn_down_b = ffn_down.astype(BF16)
    w_in_b = w_in.astype(BF16)
    gate_w_b = gate_w.astype(BF16)
    w_br_b = w_br.astype(BF16)
    w_o_b = w_o.astype(BF16)
    sg_w_b = sg_w.astype(BF16)

    xl, xc = x, ctx
    ctx_row = batch
    for l in range(DEPTH):
        last = l == DEPTH - 1
        filt_params = (hy_f_w1[l], hy_f_b1[l], hy_f_w2[l], hy_f_b2[l], hy_f_freq[l], hy_f_w3[l])
        qk_gain_row = jnp.concatenate([jnp.tile(da_qk_gain[l, 0], 2 * DA_HEADS),
                                       jnp.tile(da_qk_gain[l, 1], 2 * DA_HEADS)]).reshape(1, -1)
        ws_stack = sg_w_b[l].reshape(SG_GROUPS * SG_CHUNK, SG_CHUNK)
        bs_tile = jnp.repeat(sg_b[l].T, SG_WIDTH // SG_GROUPS, axis=1)

        head = functools.partial(_ffn_mixer, mod=mod, layer=l, norm_g=norm_g, ffn_up=ffn_up_b, ffn_down=ffn_down_b,
                                 w_in=w_in_b, qk_gain_row=qk_gain_row, cos_t=cos_t, sin_t=sin_t, ones_bd=ones_bd,
                                 ln_g=sg_ln_g[l], ln_b=sg_ln_b[l], ws_stack=ws_stack, bs_tile=bs_tile)
        xl, ql, kl, vl, hy_l, yc_l = head(xl, mod_row=None, rope=True)
        xc, qc, kc, vc, hy_c, yc_c = head(xc, mod_row=ctx_row, rope=False)

        ya_l = _attention(ql, (kl, kc), (vl, vc), da_qk_gain[l], da_lambda[l], da_subln[l], l)
        yb_l = _hyena(hy_l, hy_conv_w[l], hy_conv_b[l], filt_params, hy_skip[l], tables_l)
        tail = functools.partial(_merge_ffn, mod=mod, layer=l, norm_g=norm_g, gate_w=gate_w_b, gate_b=gate_b,
                                 w_br=w_br_b, w_o=w_o_b, ffn_up=ffn_up_b, ffn_down=ffn_down_b)
        xl = tail(xl, mod_row=None, ya=ya_l, yb=yb_l, yc=yc_l)

        if not last:
            ya_c = _attention(qc, (kc,), (vc,), da_qk_gain[l], da_lambda[l], da_subln[l], l)
            yb_c = _hyena(hy_c, hy_conv_w[l], hy_conv_b[l], filt_params, hy_skip[l], tables_c)
            xc = tail(xc, mod_row=ctx_row, ya=ya_c, yb=yb_c, yc=yc_c)
    return xl
```
